```python
import jax, jax.numpy as jnp
from jax import lax
import numpy as np

D_MODEL = 1024
BATCH = 16
SEQ = 4096
DEPTH = 1
DEC_BATCH = 32
DEC_SEQ = 32
PAST_LEN = 1024

CHUNK = 64
CONV_WIDTH = 31
D_CONV = D_MODEL
D_GMLP = D_MODEL
GMLP_CHUNK = 128
GMLP_GROUPS = 8
GMLP_GROUP_DIM = D_GMLP // GMLP_GROUPS
N_GROUPS = 4
EXPERTS_PER_GROUP = 8
N_EXPERTS = N_GROUPS * EXPERTS_PER_GROUP
TOP_K = 2
D_EXPERT = D_MODEL // 4
D_IN = 2 * D_CONV + 2 * D_GMLP + 2 * D_MODEL
EPS = 1e-6

kernel_name = "hybrid_conv_gmlp_hmoe_stream_step"


def rms_norm(x, g):
    xf = x.astype(jnp.float32)
    y = xf * lax.rsqrt(jnp.mean(xf * xf, axis=-1, keepdims=True) + EPS)
    return (y * g.astype(jnp.float32)).astype(x.dtype)


def layer_norm(x, g, b):
    xf = x.astype(jnp.float32)
    mu = jnp.mean(xf, axis=-1, keepdims=True)
    var = jnp.mean(jnp.square(xf - mu), axis=-1, keepdims=True)
    y = (xf - mu) * lax.rsqrt(var + EPS) * g.astype(jnp.float32) + b.astype(jnp.float32)
    return y.astype(x.dtype)


def modulate(h, shift, scale):
    return h * (1 + scale) + shift


def depthwise_causal_conv(x_full, w, b):
    y = lax.conv_general_dilated(
        x_full, w[:, None, :], window_strides=(1,), padding='VALID',
        dimension_numbers=('NWC', 'WIO', 'NWC'), feature_group_count=x_full.shape[-1])
    return y + b


def spatial_gate(u, v, w_s, b_s):
    B, T, _ = v.shape
    L = min(T, GMLP_CHUNK)
    n = T // L
    mask = jnp.tril(jnp.ones((L, L), dtype=bool))
    w = jnp.where(mask, w_s[:, :L, :L], 0).astype(v.dtype)
    vc = v.reshape(B, n, L, GMLP_GROUPS, GMLP_GROUP_DIM)
    mixed = jnp.einsum('hts,bnshc->bnthc', w, vc) + b_s[:, :L].T[None, None, :, :, None]
    return u * mixed.reshape(B, T, D_GMLP)


def mixer_branch(h, conv_hist, w_in, b_in, conv_w, conv_b, ln_g, ln_b, w_a,
                 v_ln_g, v_ln_b, w_s, b_s, w_b, w_o):
    z = h @ w_in + b_in
    a = z[..., :2 * D_CONV]
    uv = z[..., 2 * D_CONV:2 * D_CONV + 2 * D_GMLP]
    gates = jax.nn.sigmoid(z[..., 2 * D_CONV + 2 * D_GMLP:])
    a = jax.nn.glu(a, axis=-1)
    a_full = jnp.concatenate([conv_hist.astype(a.dtype), a], axis=1)
    new_hist = a_full[:, -(CONV_WIDTH - 1):]
    a = jax.nn.silu(layer_norm(depthwise_causal_conv(a_full, conv_w, conv_b), ln_g, ln_b))
    y_a = a @ w_a
    uv = jax.nn.gelu(uv)
    u, v = uv[..., :D_GMLP], uv[..., D_GMLP:]
    v = layer_norm(v, v_ln_g, v_ln_b)
    y_b = spatial_gate(u, v, w_s, b_s) @ w_b
    merged = gates[..., :D_MODEL] * y_a + gates[..., D_MODEL:] * y_b
    return merged @ w_o, new_hist, v


def hier_moe(h, w_rg, b_rg, w_re, b_re, w_gate, w_up, w_down):
    B, T, D = h.shape
    hf = h.reshape(-1, D)
    lg = (hf @ w_rg + b_rg).astype(jnp.float32)
    pg = jax.nn.softmax(lg, axis=-1)
    gsel = jnp.argmax(lg, axis=-1)
    pgsel = jnp.take_along_axis(pg, gsel[:, None], axis=-1)
    le = (hf @ w_re + b_re).astype(jnp.float32).reshape(-1, N_GROUPS, EXPERTS_PER_GROUP)
    le_sel = jnp.take_along_axis(le, gsel[:, None, None], axis=1)[:, 0]
    pe = jax.nn.softmax(le_sel, axis=-1)
    top_p, top_i = lax.top_k(pe, TOP_K)
    top_w = pgsel * top_p / jnp.sum(top_p, axis=-1, keepdims=True)
    eidx = gsel[:, None] * EXPERTS_PER_GROUP + top_i
    combine = jnp.einsum('nk,nke->ne', top_w,
                         jax.nn.one_hot(eidx, N_EXPERTS, dtype=jnp.float32)).astype(h.dtype)
    out = jnp.zeros_like(hf)
    for e in range(N_EXPERTS):
        he = jax.nn.silu(hf @ w_gate[e]) * (hf @ w_up[e])
        out = out + combine[:, e:e + 1] * (he @ w_down[e])
    return out.reshape(B, T, D)


def block(x, c, conv_hist, w_ada, b_ada, g_norm1, g_norm2, w_in, b_in, conv_w, conv_b,
          ln_g, ln_b, w_a, v_ln_g, v_ln_b, w_s, b_s, w_b, w_o,
          w_rg, b_rg, w_re, b_re, w_gate, w_up, w_down):
    mod = (jax.nn.silu(c) @ w_ada + b_ada)[:, None, :]
    sh1, sc1, g1, sh2, sc2, g2 = jnp.split(mod, 6, axis=-1)
    h = modulate(rms_norm(x, g_norm1), sh1, sc1)
    m, new_hist, v = mixer_branch(h, conv_hist, w_in, b_in, conv_w, conv_b, ln_g, ln_b, w_a,
                                  v_ln_g, v_ln_b, w_s, b_s, w_b, w_o)
    x = x + g1 * m
    h = modulate(rms_norm(x, g_norm2), sh2, sc2)
    x = x + g2 * hier_moe(h, w_rg, b_rg, w_re, b_re, w_gate, w_up, w_down)
    return x, new_hist, v


def setup_inputs(seed: int = 0) -> dict:
    key = jax.random.key(seed)
    ks = jax.random.split(key, 40)
    f32 = jnp.float32
    nrm = lambda k, shape, s: jax.random.normal(k, shape, f32) * s
    L = DEPTH
    return {
        "x_prompt": nrm(ks[0], (BATCH, SEQ, D_MODEL), 1.0),
        "x_sample": nrm(ks[1], (DEC_BATCH, DEC_SEQ, D_MODEL), 1.0),
        "cache_conv": nrm(ks[2], (L, DEC_BATCH, CONV_WIDTH - 1, D_CONV), 1.0),
        "c_prompt": nrm(ks[3], (BATCH, D_MODEL), 1.0),
        "c_sample": nrm(ks[4], (DEC_BATCH, D_MODEL), 1.0),
        "w_ada": nrm(ks[5], (L, D_MODEL, 6 * D_MODEL), 0.5 * D_MODEL ** -0.5),
        "b_ada": nrm(ks[6], (L, 6 * D_MODEL), 0.02),
        "g_norm1": 1.0 + nrm(ks[7], (L, D_MODEL), 0.02),
        "g_norm2": 1.0 + nrm(ks[8], (L, D_MODEL), 0.02),
        "w_in": nrm(ks[9], (L, D_MODEL, D_IN), D_MODEL ** -0.5),
        "b_in": nrm(ks[10], (L, D_IN), 0.02),
        "conv_w": nrm(ks[11], (L, CONV_WIDTH, D_CONV), CONV_WIDTH ** -0.5),
        "conv_b": nrm(ks[12], (L, D_CONV), 0.02),
        "ln_g": 1.0 + nrm(ks[13], (L, D_CONV), 0.02),
        "ln_b": nrm(ks[14], (L, D_CONV), 0.02),
        "w_a": nrm(ks[15], (L, D_CONV, D_MODEL), D_CONV ** -0.5),
        "v_ln_g": 1.0 + nrm(ks[16], (L, D_GMLP), 0.02),
        "v_ln_b": nrm(ks[17], (L, D_GMLP), 0.02),
        "w_s": nrm(ks[18], (L, GMLP_GROUPS, GMLP_CHUNK, GMLP_CHUNK), GMLP_CHUNK ** -0.5),
        "b_s": 1.0 + nrm(ks[19], (L, GMLP_GROUPS, GMLP_CHUNK), 0.1),
        "w_b": nrm(ks[20], (L, D_GMLP, D_MODEL), D_GMLP ** -0.5),
        "w_o": nrm(ks[21], (L, D_MODEL, D_MODEL), D_MODEL ** -0.5),
        "w_rg": nrm(ks[22], (L, D_MODEL, N_GROUPS), D_MODEL ** -0.5),
        "b_rg": nrm(ks[23], (L, N_GROUPS), 0.01),
        "w_re": nrm(ks[24], (L, D_MODEL, N_EXPERTS), D_MODEL ** -0.5),
        "b_re": nrm(ks[25], (L, N_EXPERTS), 0.01),
        "w_gate": nrm(ks[26], (L, N_EXPERTS, D_MODEL, D_EXPERT), D_MODEL ** -0.5),
        "w_up": nrm(ks[27], (L, N_EXPERTS, D_MODEL, D_EXPERT), D_MODEL ** -0.5),
        "w_down": nrm(ks[28], (L, N_EXPERTS, D_EXPERT, D_MODEL), D_EXPERT ** -0.5),
        "g_final": 1.0 + nrm(ks[29], (D_MODEL,), 0.02),
        "w_ada_f": nrm(ks[30], (D_MODEL, 2 * D_MODEL), 0.5 * D_MODEL ** -0.5),
        "b_ada_f": nrm(ks[31], (2 * D_MODEL,), 0.02),
    }


def reference(x_prompt, x_sample, cache_conv, c_prompt, c_sample, w_ada, b_ada, g_norm1, g_norm2,
              w_in, b_in, conv_w, conv_b, ln_g, ln_b, w_a, v_ln_g, v_ln_b, w_s, b_s, w_b, w_o,
              w_rg, b_rg, w_re, b_re, w_gate, w_up, w_down, g_final, w_ada_f, b_ada_f):
    xp, xs = x_prompt, x_sample
    conv_p, conv_s, v_s = [], [], []
    for l in range(DEPTH):
        lp = (w_ada[l], b_ada[l], g_norm1[l], g_norm2[l], w_in[l], b_in[l], conv_w[l], conv_b[l],
              ln_g[l], ln_b[l], w_a[l], v_ln_g[l], v_ln_b[l], w_s[l], b_s[l], w_b[l], w_o[l],
              w_rg[l], b_rg[l], w_re[l], b_re[l], w_gate[l], w_up[l], w_down[l])
        zero_hist = jnp.zeros((xp.shape[0], CONV_WIDTH - 1, D_CONV), xp.dtype)
        xp, hist_p, _ = block(xp, c_prompt, zero_hist, *lp)
        xs, hist_s, vs = block(xs, c_sample, cache_conv[l], *lp)
        conv_p.append(hist_p)
        conv_s.append(hist_s)
        v_s.append(vs)

    def final(x, c):
        shf, scf = jnp.split((jax.nn.silu(c) @ w_ada_f + b_ada_f)[:, None, :], 2, axis=-1)
        return modulate(rms_norm(x, g_final), shf, scf)

    y_prompt = final(xp, c_prompt)
    y_sample = final(xs, c_sample)
    state_conv_prompt = jnp.stack(conv_p, axis=0)
    state_conv_sample = jnp.stack(conv_s, axis=0)
    state_gmlp_v_sample = jnp.stack(v_s, axis=0)
    return (y_prompt, y_sample, state_conv_prompt, state_conv_sample, state_gmlp_v_sample)
```

```python
import functools

import jax
import jax.numpy as jnp
from jax import lax
from jax.experimental import pallas as pl
from jax.experimental.pallas import tpu as pltpu

F32 = jnp.float32
BF16 = jnp.bfloat16

EPS = 1e-6
CONV_WIDTH = 31
HIST = CONV_WIDTH - 1
SUBLANES = 8
LANES = 128
HIST_PAD = 32
CONV_TAPS_PAD = 32
GMLP_CHUNK = 128
GMLP_GROUPS = 8
N_GROUPS = 4
EXPERTS_PER_GROUP = 8
N_EXPERTS = N_GROUPS * EXPERTS_PER_GROUP
ROUTER_LANES = LANES
EXPERT_LANE0 = N_GROUPS
NEG_BIG = -1e30
VMEM_LIMIT_BYTES = 56 * 1024 * 1024

MIXER_TILE = 256
MOE_TILE = 1024


def _sigmoid(x):
    return 1.0 / (1.0 + jnp.exp(-x))


def _gelu_tanh(x):
    c = 0.7978845608028654
    return 0.5 * x * (1.0 + jnp.tanh(c * (x + 0.044715 * (x * x * x))))


def _rms_norm(x, g):
    return (x * lax.rsqrt(jnp.mean(x * x, axis=-1, keepdims=True) + EPS)) * g


def _layer_norm(x, g, b):
    mu = jnp.mean(x, axis=-1, keepdims=True)
    xc = x - mu
    var = jnp.mean(xc * xc, axis=-1, keepdims=True)
    return xc * lax.rsqrt(var + EPS) * g + b


def _segment_rows(ref, lo, hi, n_seg, seg):
    if n_seg == 1:
        return ref[0, :, lo:hi]
    parts = [jnp.broadcast_to(ref[g, :, lo:hi], (seg, hi - lo)) for g in range(n_seg)]
    return jnp.concatenate(parts, axis=0)


def _modulation_kernel(c_ref, w_ref, b_ref, o_ref):
    c = c_ref[...]
    s = c * _sigmoid(c)
    o_ref[...] = jnp.dot(s, w_ref[...], precision=lax.Precision.HIGHEST,
                         preferred_element_type=F32) + b_ref[...]


def _modulation(c, w, b):
    n_seq, d = c.shape
    n_out = w.shape[1]
    bn = 1024
    return pl.pallas_call(
        _modulation_kernel,
        grid=(n_out // bn,),
        in_specs=[pl.BlockSpec((n_seq, d), lambda j: (0, 0)),
                  pl.BlockSpec((d, bn), lambda j: (0, j)),
                  pl.BlockSpec((1, bn), lambda j: (0, j))],
        out_specs=pl.BlockSpec((n_seq, bn), lambda j: (0, j)),
        out_shape=jax.ShapeDtypeStruct((n_seq, n_out), F32),
    )(c, w, b.reshape(1, n_out))


def _causal_conv(a_ext, taps, seg):
    rows = seg + SUBLANES
    n_q = CONV_TAPS_PAD // SUBLANES
    shifted = [a_ext[HIST_PAD - SUBLANES * (q + 1): HIST_PAD - SUBLANES * (q + 1) + rows]
               for q in range(n_q)]
    acc = None
    for r in reversed(range(SUBLANES)):
        b_r = None
        for q in range(n_q):
            j = SUBLANES * q + r
            if j >= CONV_WIDTH:
                continue
            term = taps[j:j + 1] * shifted[q]
            b_r = term if b_r is None else b_r + term
        acc = b_r if acc is None else b_r + pltpu.roll(acc, 1, axis=0)
    return acc[SUBLANES:]


def _mixer_kernel(*refs, tile, seg, has_hist, emit_v):
    it = iter(refs)
    x_ref = next(it)
    mod_ref = next(it)
    hist_ref = next(it) if has_hist else None
    vecs_ref = next(it)
    w_in_ref = next(it)
    b_in_ref = next(it)
    taps_ref = next(it)
    w_a_ref = next(it)
    ws_ref = next(it)
    bs_ref = next(it)
    w_b_ref = next(it)
    w_o_ref = next(it)
    w_r_ref = next(it)
    b_r_ref = next(it)
    x1_ref = next(it)
    h2_ref = next(it)
    comb_ref = next(it)
    hist_out_ref = next(it)
    v_out_ref = next(it) if emit_v else None
    carry_ref = None if has_hist else next(it)

    d = x_ref.shape[-1]
    n_seg = tile // seg
    chunk = min(seg, GMLP_CHUNK)
    n_chunk = tile // chunk

    vecs = vecs_ref[...]
    conv_b, ln_g, ln_b, v_ln_g, v_ln_b, g_norm1, g_norm2 = [vecs[i:i + 1] for i in range(7)]
    mod = lambda k: _segment_rows(mod_ref, k * d, (k + 1) * d, n_seg, seg)

    x = x_ref[...]
    h = _rms_norm(x, g_norm1) * (1.0 + mod(1)) + mod(0)
    hb = h.astype(BF16)

    za = jnp.dot(hb, w_in_ref[:, 0:2 * d], preferred_element_type=F32) + b_in_ref[:, 0:2 * d]
    a = za[:, :d] * _sigmoid(za[:, d:])
    if not has_hist:
        @pl.when(pl.program_id(1) == 0)
        def _():
            carry_ref[...] = jnp.zeros_like(carry_ref)
    taps = taps_ref[...]
    conv_parts = []
    for g in range(n_seg):
        a_seg = a[g * seg:(g + 1) * seg]
        hist = hist_ref[g] if has_hist else carry_ref[...]
        conv_parts.append(_causal_conv(jnp.concatenate([hist, a_seg], axis=0), taps, seg))
        hist_out_ref[g] = a_seg[seg - HIST_PAD:]
    if not has_hist:
        carry_ref[...] = a[tile - HIST_PAD:]
    conv = conv_parts[0] if n_seg == 1 else jnp.concatenate(conv_parts, axis=0)
    ca = _layer_norm(conv + conv_b, ln_g, ln_b)
    ca = ca * _sigmoid(ca)
    y_a = jnp.dot(ca.astype(BF16), w_a_ref[...], preferred_element_type=F32)

    zuv = jnp.dot(hb, w_in_ref[:, 2 * d:4 * d], preferred_element_type=F32) + b_in_ref[:, 2 * d:4 * d]
    guv = _gelu_tanh(zuv)
    u = guv[:, :d]
    v = _layer_norm(guv[:, d:], v_ln_g, v_ln_b)
    if emit_v:
        v_out_ref[...] = v
    vb = v.astype(BF16)
    gd = d // GMLP_GROUPS
    causal = (lax.broadcasted_iota(jnp.int32, (chunk, chunk), 0)
              >= lax.broadcasted_iota(jnp.int32, (chunk, chunk), 1))
    mixed_cols = []
    for hg in range(GMLP_GROUPS):
        w = jnp.where(causal, ws_ref[hg], 0.0).astype(BF16)
        rhs = [vb[c * chunk:(c + 1) * chunk, hg * gd:(hg + 1) * gd] for c in range(n_chunk)]
        rhs = rhs[0] if n_chunk == 1 else jnp.concatenate(rhs, axis=1)
        out = jnp.dot(w, rhs, preferred_element_type=F32)
        cols = [out[:, c * gd:(c + 1) * gd] for c in range(n_chunk)]
        mixed_cols.append(cols[0] if n_chunk == 1 else jnp.concatenate(cols, axis=0))
    mixed = jnp.concatenate(mixed_cols, axis=1)
    bs = bs_ref[...]
    mixed = mixed + (bs if n_chunk == 1 else jnp.concatenate([bs] * n_chunk, axis=0))
    y_b = jnp.dot((u * mixed).astype(BF16), w_b_ref[...], preferred_element_type=F32)

    zg = jnp.dot(hb, w_in_ref[:, 4 * d:6 * d], preferred_element_type=F32) + b_in_ref[:, 4 * d:6 * d]
    gates = _sigmoid(zg)
    merged = gates[:, :d] * y_a + gates[:, d:] * y_b
    m = jnp.dot(merged.astype(BF16), w_o_ref[...], preferred_element_type=F32)
    x1 = x + mod(2) * m
    x1_ref[...] = x1

    h2 = _rms_norm(x1, g_norm2) * (1.0 + mod(4)) + mod(3)
    h2_ref[...] = h2.astype(BF16)
    logits = jnp.dot(h2, w_r_ref[...], precision=lax.Precision.HIGHEST,
                     preferred_element_type=F32) + b_r_ref[...]
    lane = lax.broadcasted_iota(jnp.int32, (tile, ROUTER_LANES), 1).astype(F32)
    is_group = lane < N_GROUPS
    lg = jnp.where(is_group, logits, NEG_BIG)
    mg = jnp.max(lg, axis=1, keepdims=True)
    gsel = jnp.min(jnp.where(lg == mg, lane, float(ROUTER_LANES)), axis=1, keepdims=True)
    pgsel = 1.0 / jnp.sum(jnp.where(is_group, jnp.exp(lg - mg), 0.0), axis=1, keepdims=True)
    lo = EXPERT_LANE0 + EXPERTS_PER_GROUP * gsel
    le = jnp.where((lane >= lo) & (lane < lo + EXPERTS_PER_GROUP), logits, NEG_BIG)
    m1 = jnp.max(le, axis=1, keepdims=True)
    i1 = jnp.min(jnp.where(le == m1, lane, float(ROUTER_LANES)), axis=1, keepdims=True)
    le2 = jnp.where(lane == i1, NEG_BIG, le)
    m2 = jnp.max(le2, axis=1, keepdims=True)
    i2 = jnp.min(jnp.where(le2 == m2, lane, float(ROUTER_LANES)), axis=1, keepdims=True)
    e2 = jnp.exp(m2 - m1)
    w1 = pgsel / (1.0 + e2)
    w2 = w1 * e2
    comb_ref[...] = jnp.where(lane == i1, w1, 0.0) + jnp.where(lane == i2, w2, 0.0)


def _const_spec(shape):
    nd = len(shape)
    return pl.BlockSpec(shape, lambda i, t: (0,) * nd, pipeline_mode=pl.Buffered(1))


def _mixer(x, mod, hist, wts, *, tile, seg, emit_v):
    nb, s, d = x.shape
    n_seg = tile // seg
    has_hist = hist is not None
    grid = (nb, s // tile)
    chunk = min(seg, GMLP_CHUNK)

    in_specs = [pl.BlockSpec((None, tile, d), lambda i, t: (i, t, 0)),
                pl.BlockSpec((n_seg, 1, mod.shape[-1]), lambda i, t: (i, 0, 0))]
    args = [x, mod]
    if has_hist:
        in_specs.append(pl.BlockSpec((n_seg, HIST_PAD, d), lambda i, t: (i, 0, 0)))
        args.append(hist)
    consts = [wts["vecs"], wts["w_in"], wts["b_in"], wts["taps"], wts["w_a"],
              wts["w_s"][:, :chunk, :chunk], wts["b_s"][:chunk], wts["w_b"], wts["w_o"],
              wts["w_r"], wts["b_r"]]
    in_specs += [_const_spec(c.shape) for c in consts]
    args += consts

    out_shape = [jax.ShapeDtypeStruct((nb, s, d), F32),
                 jax.ShapeDtypeStruct((nb, s, d), BF16),
                 jax.ShapeDtypeStruct((nb, s, ROUTER_LANES), F32),
                 jax.ShapeDtypeStruct((nb * n_seg, HIST_PAD, d), F32)]
    out_specs = [pl.BlockSpec((None, tile, d), lambda i, t: (i, t, 0)),
                 pl.BlockSpec((None, tile, d), lambda i, t: (i, t, 0)),
                 pl.BlockSpec((None, tile, ROUTER_LANES), lambda i, t: (i, t, 0)),
                 pl.BlockSpec((n_seg, HIST_PAD, d), lambda i, t: (i, 0, 0))]
    if emit_v:
        out_shape.append(jax.ShapeDtypeStruct((nb, s, d), F32))
        out_specs.append(pl.BlockSpec((None, tile, d), lambda i, t: (i, t, 0)))
    scratch = [] if has_hist else [pltpu.VMEM((HIST_PAD, d), F32)]

    return pl.pallas_call(
        functools.partial(_mixer_kernel, tile=tile, seg=seg, has_hist=has_hist, emit_v=emit_v),
        grid=grid, in_specs=in_specs, out_specs=out_specs, out_shape=out_shape,
        scratch_shapes=scratch,
        compiler_params=pltpu.CompilerParams(
            dimension_semantics=("arbitrary", "arbitrary"),
            vmem_limit_bytes=VMEM_LIMIT_BYTES),
    )(*args)


def _moe_kernel(h2_ref, comb_ref, x1_ref, mod_ref, modf_ref, gfin_ref, wg_ref, wu_ref, wd_ref,
                y_ref, acc_ref, *, tile, seg):
    e = pl.program_id(2)
    d = x1_ref.shape[-1]
    n_seg = tile // seg

    @pl.when(e == 0)
    def _():
        acc_ref[...] = jnp.zeros_like(acc_ref)

    hb = h2_ref[...]
    hg = jnp.dot(hb, wg_ref[...], preferred_element_type=F32)
    hu = jnp.dot(hb, wu_ref[...], preferred_element_type=F32)
    he = (hg * _sigmoid(hg)) * hu
    ye = jnp.dot(he.astype(BF16), wd_ref[...], preferred_element_type=F32)
    lane = lax.broadcasted_iota(jnp.int32, (tile, ROUTER_LANES), 1)
    cw = jnp.sum(jnp.where(lane == e + EXPERT_LANE0, comb_ref[...], 0.0), axis=1, keepdims=True)
    acc_ref[...] += cw * ye

    @pl.when(e == N_EXPERTS - 1)
    def _():
        g2 = _segment_rows(mod_ref, 5 * d, 6 * d, n_seg, seg)
        shf = _segment_rows(modf_ref, 0, d, n_seg, seg)
        scf = _segment_rows(modf_ref, d, 2 * d, n_seg, seg)
        x2 = x1_ref[...] + g2 * acc_ref[...]
        y_ref[...] = _rms_norm(x2, gfin_ref[...]) * (1.0 + scf) + shf


def _moe(h2, comb, x1, mod, modf, wts, *, tile, seg):
    nb, s, d = x1.shape
    n_seg = tile // seg
    de = wts["w_gate"].shape[-1]
    tok = lambda i, t, e: (i, t, 0)
    per_seq = lambda i, t, e: (i, 0, 0)
    return pl.pallas_call(
        functools.partial(_moe_kernel, tile=tile, seg=seg),
        grid=(nb, s // tile, N_EXPERTS),
        in_specs=[pl.BlockSpec((None, tile, d), tok),
                  pl.BlockSpec((None, tile, ROUTER_LANES), tok),
                  pl.BlockSpec((None, tile, d), tok),
                  pl.BlockSpec((n_seg, 1, mod.shape[-1]), per_seq),
                  pl.BlockSpec((n_seg, 1, modf.shape[-1]), per_seq),
                  pl.BlockSpec((1, d), lambda i, t, e: (0, 0)),
                  pl.BlockSpec((None, d, de), lambda i, t, e: (e, 0, 0)),
                  pl.BlockSpec((None, d, de), lambda i, t, e: (e, 0, 0)),
                  pl.BlockSpec((None, de, d), lambda i, t, e: (e, 0, 0))],
        out_specs=pl.BlockSpec((None, tile, d), tok),
        out_shape=jax.ShapeDtypeStruct((nb, s, d), F32),
        scratch_shapes=[pltpu.VMEM((tile, d), F32)],
        compiler_params=pltpu.CompilerParams(
            dimension_semantics=("arbitrary", "arbitrary", "arbitrary"),
            vmem_limit_bytes=VMEM_LIMIT_BYTES),
    )(h2, comb, x1, mod, modf, wts["g_final"], wts["w_gate"], wts["w_up"], wts["w_down"])


def _layer_weights(l, w_in, b_in, conv_w, conv_b, ln_g, ln_b, w_a, v_ln_g, v_ln_b, w_s, b_s,
                   w_b, w_o, g_norm1, g_norm2, w_rg, b_rg, w_re, b_re, w_gate, w_up, w_down,
                   g_final):
    d = w_in.shape[1]
    gd = d // GMLP_GROUPS
    zero_row = jnp.zeros((1, d), F32)
    vecs = jnp.stack([conv_b[l], ln_g[l], ln_b[l], v_ln_g[l], v_ln_b[l], g_norm1[l], g_norm2[l],
                      zero_row[0]])
    taps = jnp.concatenate([conv_w[l][::-1], jnp.zeros((CONV_TAPS_PAD - CONV_WIDTH, d), F32)])
    pad = ROUTER_LANES - N_GROUPS - N_EXPERTS
    w_r = jnp.concatenate([w_rg[l], w_re[l], jnp.zeros((d, pad), F32)], axis=1)
    b_r = jnp.concatenate([b_rg[l], b_re[l], jnp.zeros((pad,), F32)]).reshape(1, ROUTER_LANES)
    return dict(
        vecs=vecs, taps=taps,
        w_in=w_in[l].astype(BF16), b_in=b_in[l].reshape(1, -1),
        w_a=w_a[l].astype(BF16), w_b=w_b[l].astype(BF16), w_o=w_o[l].astype(BF16),
        w_s=w_s[l], b_s=jnp.repeat(b_s[l].T, gd, axis=1),
        w_r=w_r, b_r=b_r,
        w_gate=w_gate[l].astype(BF16), w_up=w_up[l].astype(BF16), w_down=w_down[l].astype(BF16),
        g_final=g_final.reshape(1, d))


def kernel(x_prompt, x_sample, cache_conv, c_prompt, c_sample, w_ada, b_ada, g_norm1, g_norm2, w_in, b_in, conv_w, conv_b, ln_g, ln_b, w_a, v_ln_g, v_ln_b, w_s, b_s, w_b, w_o, w_rg, b_rg, w_re, b_re, w_gate, w_up, w_down, g_final, w_ada_f, b_ada_f):
    depth = w_in.shape[0]
    assert depth == 1, "the streaming step is written for a single layer"
    l = 0
    batch, seq, d = x_prompt.shape
    dec_batch, dec_seq, _ = x_sample.shape
    assert dec_seq == HIST_PAD and seq % MIXER_TILE == 0 and (dec_batch * dec_seq) % MIXER_TILE == 0

    wts = _layer_weights(l, w_in, b_in, conv_w, conv_b, ln_g, ln_b, w_a, v_ln_g, v_ln_b, w_s, b_s,
                         w_b, w_o, g_norm1, g_norm2, w_rg, b_rg, w_re, b_re, w_gate, w_up, w_down,
                         g_final)

    c_all = jnp.concatenate([c_prompt, c_sample], axis=0)
    mod = _modulation(c_all, w_ada[l], b_ada[l])[:, None, :]
    modf = _modulation(c_all, w_ada_f, b_ada_f)[:, None, :]

    x1p, h2p, combp, histp = _mixer(x_prompt, mod[:batch], None, wts,
                                    tile=MIXER_TILE, seg=MIXER_TILE, emit_v=False)
    moe_tile_p = min(MOE_TILE, seq)
    y_prompt = _moe(h2p, combp, x1p, mod[:batch], modf[:batch], wts, tile=moe_tile_p, seg=moe_tile_p)

    n_tok = dec_batch * dec_seq
    xs = x_sample.reshape(n_tok // MIXER_TILE, MIXER_TILE, d)
    hist_s = jnp.pad(cache_conv[l], ((0, 0), (HIST_PAD - HIST, 0), (0, 0)))
    x1s, h2s, combs, hists, vs = _mixer(xs, mod[batch:], hist_s, wts,
                                        tile=MIXER_TILE, seg=dec_seq, emit_v=True)
    moe_tile_s = min(MOE_TILE, n_tok)
    y_sample = _moe(h2s.reshape(n_tok // moe_tile_s, moe_tile_s, d),
                    combs.reshape(n_tok // moe_tile_s, moe_tile_s, ROUTER_LANES),
                    x1s.reshape(n_tok // moe_tile_s, moe_tile_s, d),
                    mod[batch:], modf[batch:], wts, tile=moe_tile_s, seg=dec_seq)

    state_conv_prompt = histp[None, :, HIST_PAD - HIST:, :]
    state_conv_sample = hists[None, :, HIST_PAD - HIST:, :]
    state_gmlp_v_sample = vs.reshape(1, dec_batch, dec_seq, d)
    return (y_prompt, y_sample.reshape(dec_batch, dec_seq, d), state_conv_prompt,
            state_conv_sample, state_gmlp_v_sample)
```

```python
import functools

import jax
import jax.numpy as jnp
import numpy as np
from jax import lax
from jax.experimental import pallas as pl
from jax.experimental.pallas import tpu as pltpu

F32 = jnp.float32
BF16 = jnp.bfloat16

EPS = 1e-6
CONV_WIDTH = 31
HIST = CONV_WIDTH - 1
SUBLANES = 8
LANES = 128
HIST_PAD = 32
CONV_TAPS_PAD = 32
GMLP_CHUNK = 128
GMLP_GROUPS = 8
N_GROUPS = 4
EXPERTS_PER_GROUP = 8
N_EXPERTS = N_GROUPS * EXPERTS_PER_GROUP
ROUTER_LANES = LANES
EXPERT_LANE0 = N_GROUPS
NEG_BIG = -1e30
VMEM_LIMIT_BYTES = 56 * 1024 * 1024

MIXER_TILE = 256
MOE_TILE = 1024
SORT_TILE = 256

PAIRS_PER_GROUP = EXPERTS_PER_GROUP * (EXPERTS_PER_GROUP - 1) // 2
N_BUCKETS = N_GROUPS * PAIRS_PER_GROUP
BUCKET_LANE = 64
WA_LANE = 65
WB_LANE = 66
_BUCKET_EXPERTS = np.array([(g * EXPERTS_PER_GROUP + a, g * EXPERTS_PER_GROUP + b)
                            for g in range(N_GROUPS)
                            for a in range(EXPERTS_PER_GROUP)
                            for b in range(a + 1, EXPERTS_PER_GROUP)], dtype=np.int32)


def _sigmoid(x):
    return 1.0 / (1.0 + jnp.exp(-x))


def _gelu_tanh(x):
    c = 0.7978845608028654
    return 0.5 * x * (1.0 + jnp.tanh(c * (x + 0.044715 * (x * x * x))))


def _rms_norm(x, g):
    return (x * lax.rsqrt(jnp.mean(x * x, axis=-1, keepdims=True) + EPS)) * g


def _layer_norm(x, g, b):
    mu = jnp.mean(x, axis=-1, keepdims=True)
    xc = x - mu
    var = jnp.mean(xc * xc, axis=-1, keepdims=True)
    return xc * lax.rsqrt(var + EPS) * g + b


def _segment_rows(ref, lo, hi, n_seg, seg):
    if n_seg == 1:
        return ref[0, :, lo:hi]
    parts = [jnp.broadcast_to(ref[g, :, lo:hi], (seg, hi - lo)) for g in range(n_seg)]
    return jnp.concatenate(parts, axis=0)


def _modulation_kernel(c_ref, w_ref, b_ref, o_ref):
    c = c_ref[...]
    s = c * _sigmoid(c)
    o_ref[...] = jnp.dot(s, w_ref[...], precision=lax.Precision.HIGHEST,
                         preferred_element_type=F32) + b_ref[...]


def _modulation(c, w, b):
    n_seq, d = c.shape
    n_out = w.shape[1]
    bn = 1024
    return pl.pallas_call(
        _modulation_kernel,
        grid=(n_out // bn,),
        in_specs=[pl.BlockSpec((n_seq, d), lambda j: (0, 0)),
                  pl.BlockSpec((d, bn), lambda j: (0, j)),
                  pl.BlockSpec((1, bn), lambda j: (0, j))],
        out_specs=pl.BlockSpec((n_seq, bn), lambda j: (0, j)),
        out_shape=jax.ShapeDtypeStruct((n_seq, n_out), F32),
    )(c, w, b.reshape(1, n_out))


def _causal_conv(a_ext, taps, seg):
    rows = seg + SUBLANES
    n_q = CONV_TAPS_PAD // SUBLANES
    shifted = [a_ext[HIST_PAD - SUBLANES * (q + 1): HIST_PAD - SUBLANES * (q + 1) + rows]
               for q in range(n_q)]
    acc = None
    for r in reversed(range(SUBLANES)):
        b_r = None
        for q in range(n_q):
            j = SUBLANES * q + r
            if j >= CONV_WIDTH:
                continue
            term = taps[j:j + 1] * shifted[q]
            b_r = term if b_r is None else b_r + term
        acc = b_r if acc is None else b_r + pltpu.roll(acc, 1, axis=0)
    return acc[SUBLANES:]


def _mixer_kernel(*refs, tile, seg, has_hist, emit_v):
    it = iter(refs)
    x_ref = next(it)
    mod_ref = next(it)
    hist_ref = next(it) if has_hist else None
    vecs_ref = next(it)
    w_in_ref = next(it)
    b_in_ref = next(it)
    taps_ref = next(it)
    w_a_ref = next(it)
    ws_ref = next(it)
    bs_ref = next(it)
    w_b_ref = next(it)
    w_o_ref = next(it)
    w_r_ref = next(it)
    b_r_ref = next(it)
    x1_ref = next(it)
    h2_ref = next(it)
    route_ref = next(it)
    hist_out_ref = next(it)
    count_ref = next(it)
    v_out_ref = next(it) if emit_v else None
    count_acc_ref = next(it)
    carry_ref = None if has_hist else next(it)

    d = x_ref.shape[-1]
    n_seg = tile // seg
    chunk = min(seg, GMLP_CHUNK)
    n_chunk = tile // chunk

    vecs = vecs_ref[...]
    conv_b, ln_g, ln_b, v_ln_g, v_ln_b, g_norm1, g_norm2 = [vecs[i:i + 1] for i in range(7)]
    mod = lambda k: _segment_rows(mod_ref, k * d, (k + 1) * d, n_seg, seg)

    x = x_ref[...]
    h = _rms_norm(x, g_norm1) * (1.0 + mod(1)) + mod(0)
    hb = h.astype(BF16)

    za = jnp.dot(hb, w_in_ref[:, 0:2 * d], preferred_element_type=F32) + b_in_ref[:, 0:2 * d]
    a = za[:, :d] * _sigmoid(za[:, d:])
    if not has_hist:
        @pl.when(pl.program_id(1) == 0)
        def _():
            carry_ref[...] = jnp.zeros_like(carry_ref)
    taps = taps_ref[...]
    conv_parts = []
    for g in range(n_seg):
        a_seg = a[g * seg:(g + 1) * seg]
        hist = hist_ref[g] if has_hist else carry_ref[...]
        conv_parts.append(_causal_conv(jnp.concatenate([hist, a_seg], axis=0), taps, seg))
        hist_out_ref[g] = a_seg[seg - HIST_PAD:]
    if not has_hist:
        carry_ref[...] = a[tile - HIST_PAD:]
    conv = conv_parts[0] if n_seg == 1 else jnp.concatenate(conv_parts, axis=0)
    ca = _layer_norm(conv + conv_b, ln_g, ln_b)
    ca = ca * _sigmoid(ca)
    y_a = jnp.dot(ca.astype(BF16), w_a_ref[...], preferred_element_type=F32)

    zuv = jnp.dot(hb, w_in_ref[:, 2 * d:4 * d], preferred_element_type=F32) + b_in_ref[:, 2 * d:4 * d]
    guv = _gelu_tanh(zuv)
    u = guv[:, :d]
    v = _layer_norm(guv[:, d:], v_ln_g, v_ln_b)
    if emit_v:
        v_out_ref[...] = v
    vb = v.astype(BF16)
    gd = d // GMLP_GROUPS
    causal = (lax.broadcasted_iota(jnp.int32, (chunk, chunk), 0)
              >= lax.broadcasted_iota(jnp.int32, (chunk, chunk), 1))
    mixed_cols = []
    for hg in range(GMLP_GROUPS):
        w = jnp.where(causal, ws_ref[hg], 0.0).astype(BF16)
        rhs = [vb[c * chunk:(c + 1) * chunk, hg * gd:(hg + 1) * gd] for c in range(n_chunk)]
        rhs = rhs[0] if n_chunk == 1 else jnp.concatenate(rhs, axis=1)
        out = jnp.dot(w, rhs, preferred_element_type=F32)
        cols = [out[:, c * gd:(c + 1) * gd] for c in range(n_chunk)]
        mixed_cols.append(cols[0] if n_chunk == 1 else jnp.concatenate(cols, axis=0))
    mixed = jnp.concatenate(mixed_cols, axis=1)
    bs = bs_ref[...]
    mixed = mixed + (bs if n_chunk == 1 else jnp.concatenate([bs] * n_chunk, axis=0))
    y_b = jnp.dot((u * mixed).astype(BF16), w_b_ref[...], preferred_element_type=F32)

    zg = jnp.dot(hb, w_in_ref[:, 4 * d:6 * d], preferred_element_type=F32) + b_in_ref[:, 4 * d:6 * d]
    gates = _sigmoid(zg)
    merged = gates[:, :d] * y_a + gates[:, d:] * y_b
    m = jnp.dot(merged.astype(BF16), w_o_ref[...], preferred_element_type=F32)
    x1 = x + mod(2) * m
    x1_ref[...] = x1

    h2 = _rms_norm(x1, g_norm2) * (1.0 + mod(4)) + mod(3)
    h2_ref[...] = h2.astype(h2_ref.dtype)
    logits = jnp.dot(h2.astype(BF16), w_r_ref[...], preferred_element_type=F32) + b_r_ref[...]
    lane = lax.broadcasted_iota(jnp.int32, (tile, ROUTER_LANES), 1).astype(F32)
    is_group = lane < N_GROUPS
    lg = jnp.where(is_group, logits, NEG_BIG)
    mg = jnp.max(lg, axis=1, keepdims=True)
    gsel = jnp.min(jnp.where(lg == mg, lane, float(ROUTER_LANES)), axis=1, keepdims=True)
    pgsel = 1.0 / jnp.sum(jnp.where(is_group, jnp.exp(lg - mg), 0.0), axis=1, keepdims=True)
    lo = EXPERT_LANE0 + EXPERTS_PER_GROUP * gsel
    le = jnp.where((lane >= lo) & (lane < lo + EXPERTS_PER_GROUP), logits, NEG_BIG)
    m1 = jnp.max(le, axis=1, keepdims=True)
    i1 = jnp.min(jnp.where(le == m1, lane, float(ROUTER_LANES)), axis=1, keepdims=True)
    le2 = jnp.where(lane == i1, NEG_BIG, le)
    m2 = jnp.max(le2, axis=1, keepdims=True)
    i2 = jnp.min(jnp.where(le2 == m2, lane, float(ROUTER_LANES)), axis=1, keepdims=True)
    e2 = jnp.exp(m2 - m1)
    w1 = pgsel / (1.0 + e2)
    w2 = w1 * e2
    first_lower = i1 < i2
    ea = jnp.where(first_lower, i1, i2) - lo
    eb = jnp.where(first_lower, i2, i1) - lo
    bucket = gsel * PAIRS_PER_GROUP + ea * (2 * EXPERTS_PER_GROUP - 1 - ea) * 0.5 + (eb - ea - 1.0)
    wa = jnp.where(first_lower, w1, w2)
    wb = jnp.where(first_lower, w2, w1)
    route_ref[...] = (jnp.where(lane == i1, w1, 0.0) + jnp.where(lane == i2, w2, 0.0)
                      + jnp.where(lane == BUCKET_LANE, bucket, 0.0)
                      + jnp.where(lane == WA_LANE, wa, 0.0) + jnp.where(lane == WB_LANE, wb, 0.0))

    @pl.when((pl.program_id(0) == 0) & (pl.program_id(1) == 0))
    def _():
        count_acc_ref[...] = jnp.zeros_like(count_acc_ref)
    count_acc_ref[...] += jnp.sum(jnp.where(lane == bucket, 1.0, 0.0), axis=0, keepdims=True)
    count_ref[...] = count_acc_ref[...]


def _const_spec(shape):
    nd = len(shape)
    return pl.BlockSpec(shape, lambda i, t: (0,) * nd, pipeline_mode=pl.Buffered(1))


def _mixer(x, mod, hist, wts, *, tile, seg, emit_v, h2_dtype):
    nb, s, d = x.shape
    n_seg = tile // seg
    has_hist = hist is not None
    grid = (nb, s // tile)
    chunk = min(seg, GMLP_CHUNK)

    in_specs = [pl.BlockSpec((None, tile, d), lambda i, t: (i, t, 0)),
                pl.BlockSpec((n_seg, 1, mod.shape[-1]), lambda i, t: (i, 0, 0))]
    args = [x, mod]
    if has_hist:
        in_specs.append(pl.BlockSpec((n_seg, HIST_PAD, d), lambda i, t: (i, 0, 0)))
        args.append(hist)
    consts = [wts["vecs"], wts["w_in"], wts["b_in"], wts["taps"], wts["w_a"],
              wts["w_s"][:, :chunk, :chunk], wts["b_s"][:chunk], wts["w_b"], wts["w_o"],
              wts["w_r"], wts["b_r"]]
    in_specs += [_const_spec(c.shape) for c in consts]
    args += consts

    out_shape = [jax.ShapeDtypeStruct((nb, s, d), F32),
                 jax.ShapeDtypeStruct((nb, s, d), h2_dtype),
                 jax.ShapeDtypeStruct((nb, s, ROUTER_LANES), F32),
                 jax.ShapeDtypeStruct((nb * n_seg, HIST_PAD, d), F32),
                 jax.ShapeDtypeStruct((SUBLANES, ROUTER_LANES), F32)]
    out_specs = [pl.BlockSpec((None, tile, d), lambda i, t: (i, t, 0)),
                 pl.BlockSpec((None, tile, d), lambda i, t: (i, t, 0)),
                 pl.BlockSpec((None, tile, ROUTER_LANES), lambda i, t: (i, t, 0)),
                 pl.BlockSpec((n_seg, HIST_PAD, d), lambda i, t: (i, 0, 0)),
                 pl.BlockSpec((SUBLANES, ROUTER_LANES), lambda i, t: (0, 0))]
    if emit_v:
        out_shape.append(jax.ShapeDtypeStruct((nb, s, d), F32))
        out_specs.append(pl.BlockSpec((None, tile, d), lambda i, t: (i, t, 0)))
    scratch = [pltpu.VMEM((SUBLANES, ROUTER_LANES), F32)]
    if not has_hist:
        scratch.append(pltpu.VMEM((HIST_PAD, d), F32))

    return pl.pallas_call(
        functools.partial(_mixer_kernel, tile=tile, seg=seg, has_hist=has_hist, emit_v=emit_v),
        grid=grid, in_specs=in_specs, out_specs=out_specs, out_shape=out_shape,
        scratch_shapes=scratch,
        compiler_params=pltpu.CompilerParams(
            dimension_semantics=("arbitrary", "arbitrary"),
            vmem_limit_bytes=VMEM_LIMIT_BYTES),
    )(*args)


def _moe_kernel(h2_ref, comb_ref, x1_ref, mod_ref, modf_ref, gfin_ref, wg_ref, wu_ref, wd_ref,
                y_ref, acc_ref, *, tile, seg):
    e = pl.program_id(2)
    d = x1_ref.shape[-1]
    n_seg = tile // seg

    @pl.when(e == 0)
    def _():
        acc_ref[...] = jnp.zeros_like(acc_ref)

    hb = h2_ref[...]
    hg = jnp.dot(hb, wg_ref[...], preferred_element_type=F32)
    hu = jnp.dot(hb, wu_ref[...], preferred_element_type=F32)
    he = (hg * _sigmoid(hg)) * hu
    ye = jnp.dot(he.astype(BF16), wd_ref[...], preferred_element_type=F32)
    lane = lax.broadcasted_iota(jnp.int32, (tile, ROUTER_LANES), 1)
    cw = jnp.sum(jnp.where(lane == e + EXPERT_LANE0, comb_ref[...], 0.0), axis=1, keepdims=True)
    acc_ref[...] += cw * ye

    @pl.when(e == N_EXPERTS - 1)
    def _():
        g2 = _segment_rows(mod_ref, 5 * d, 6 * d, n_seg, seg)
        shf = _segment_rows(modf_ref, 0, d, n_seg, seg)
        scf = _segment_rows(modf_ref, d, 2 * d, n_seg, seg)
        x2 = x1_ref[...] + g2 * acc_ref[...]
        y_ref[...] = _rms_norm(x2, gfin_ref[...]) * (1.0 + scf) + shf


def _moe(h2, comb, x1, mod, modf, wts, *, tile, seg):
    nb, s, d = x1.shape
    n_seg = tile // seg
    de = wts["w_gate"].shape[-1]
    tok = lambda i, t, e: (i, t, 0)
    per_seq = lambda i, t, e: (i, 0, 0)
    return pl.pallas_call(
        functools.partial(_moe_kernel, tile=tile, seg=seg),
        grid=(nb, s // tile, N_EXPERTS),
        in_specs=[pl.BlockSpec((None, tile, d), tok),
                  pl.BlockSpec((None, tile, ROUTER_LANES), tok),
                  pl.BlockSpec((None, tile, d), tok),
                  pl.BlockSpec((n_seg, 1, mod.shape[-1]), per_seq),
                  pl.BlockSpec((n_seg, 1, modf.shape[-1]), per_seq),
                  pl.BlockSpec((1, d), lambda i, t, e: (0, 0)),
                  pl.BlockSpec((None, d, de), lambda i, t, e: (e, 0, 0)),
                  pl.BlockSpec((None, d, de), lambda i, t, e: (e, 0, 0)),
                  pl.BlockSpec((None, de, d), lambda i, t, e: (e, 0, 0))],
        out_specs=pl.BlockSpec((None, tile, d), tok),
        out_shape=jax.ShapeDtypeStruct((nb, s, d), F32),
        scratch_shapes=[pltpu.VMEM((tile, d), F32)],
        compiler_params=pltpu.CompilerParams(
            dimension_semantics=("arbitrary", "arbitrary", "arbitrary"),
            vmem_limit_bytes=VMEM_LIMIT_BYTES),
    )(h2, comb, x1, mod, modf, wts["g_final"], wts["w_gate"], wts["w_up"], wts["w_down"])


def _plan_kernel(route_ref, start_ref, slot_ref, next_ref, *, tile):
    @pl.when(pl.program_id(0) == 0)
    def _():
        next_ref[...] = start_ref[...]

    bucket_row = route_ref[...].T[BUCKET_LANE:BUCKET_LANE + 1, :]
    rows = lax.broadcasted_iota(jnp.int32, (ROUTER_LANES, tile), 0).astype(F32)
    member = rows == bucket_row
    upper = (lax.broadcasted_iota(jnp.int32, (tile, tile), 0)
             <= lax.broadcasted_iota(jnp.int32, (tile, tile), 1))
    seen = jnp.dot(jnp.where(member, 1.0, 0.0).astype(BF16), jnp.where(upper, 1.0, 0.0).astype(BF16),
                   preferred_element_type=F32)
    nxt = next_ref[...]
    slot = jnp.sum(jnp.where(member, seen - 1.0 + nxt[:, 0:1], 0.0), axis=0, keepdims=True)
    slot_ref[...] = slot.astype(jnp.int32)
    next_ref[...] = nxt + seen[:, tile - 1:tile]


def _plan(route, bucket_start, *, tile):
    n = route.shape[0]
    start = jnp.broadcast_to(bucket_start[:, None], (ROUTER_LANES, ROUTER_LANES))
    slots = pl.pallas_call(
        functools.partial(_plan_kernel, tile=tile),
        grid=(n // tile,),
        in_specs=[pl.BlockSpec((tile, ROUTER_LANES), lambda t: (t, 0)),
                  pl.BlockSpec((ROUTER_LANES, ROUTER_LANES), lambda t: (0, 0))],
        out_specs=pl.BlockSpec((None, 1, tile), lambda t: (t, 0, 0)),
        out_shape=jax.ShapeDtypeStruct((n // tile, 1, tile), jnp.int32),
        scratch_shapes=[pltpu.VMEM((ROUTER_LANES, ROUTER_LANES), F32)],
        compiler_params=pltpu.CompilerParams(dimension_semantics=("arbitrary",)),
    )(route, start)
    return slots.reshape(n)


def _rows_copy(src_hbm, dst_hbm, sem, n_rows):
    return pltpu.make_async_copy(src_hbm.at[pl.ds(0, n_rows)], dst_hbm.at[pl.ds(0, n_rows)], sem)


def _dispatch_kernel(slot_ref, src_hbm, zeros_hbm, dst_hbm, sems, *, tile, n_steps):
    del zeros_hbm
    t = pl.program_id(0)
    base = t * tile

    def issue(r, carry):
        pltpu.make_async_copy(src_hbm.at[pl.ds(base + r, 1)], dst_hbm.at[pl.ds(slot_ref[base + r], 1)],
                              sems.at[t % 2]).start()
        return carry
    lax.fori_loop(0, tile, issue, 0, unroll=8)

    @pl.when(t > 0)
    def _():
        _rows_copy(src_hbm, dst_hbm, sems.at[(t - 1) % 2], tile).wait()

    @pl.when(t == n_steps - 1)
    def _():
        _rows_copy(src_hbm, dst_hbm, sems.at[t % 2], tile).wait()


def _dispatch(slots, src, n_slots, *, tile):
    n, d = src.shape
    n_steps = n // tile
    return pl.pallas_call(
        functools.partial(_dispatch_kernel, tile=tile, n_steps=n_steps),
        grid_spec=pltpu.PrefetchScalarGridSpec(
            num_scalar_prefetch=1, grid=(n_steps,),
            in_specs=[pl.BlockSpec(memory_space=pl.ANY), pl.BlockSpec(memory_space=pl.ANY)],
            out_specs=pl.BlockSpec(memory_space=pl.ANY),
            scratch_shapes=[pltpu.SemaphoreType.DMA((2,))]),
        out_shape=jax.ShapeDtypeStruct((n_slots, d), src.dtype),
        input_output_aliases={2: 0},
        compiler_params=pltpu.CompilerParams(dimension_semantics=("arbitrary",)),
    )(slots, src, jnp.zeros((n_slots, d), src.dtype))


def _grouped_moe_kernel(ea_ref, eb_ref, n_used_ref, hs_ref, wga_ref, wua_ref, wda_ref,
                        wgb_ref, wub_ref, wdb_ref, ys_ref):
    del ea_ref, eb_ref
    d = hs_ref.shape[-1]
    j = pl.program_id(0)

    @pl.when(j < n_used_ref[0])
    def _():
        xb = hs_ref[...].astype(BF16)
        for k, (wg, wu, wd) in enumerate(((wga_ref, wua_ref, wda_ref), (wgb_ref, wub_ref, wdb_ref))):
            g = jnp.dot(xb, wg[...], preferred_element_type=F32)
            u = jnp.dot(xb, wu[...], preferred_element_type=F32)
            he = (g * _sigmoid(g)) * u
            ys_ref[:, k * d:(k + 1) * d] = jnp.dot(he.astype(BF16), wd[...], preferred_element_type=F32)

    @pl.when(j >= n_used_ref[0])
    def _():
        ys_ref[...] = jnp.zeros_like(ys_ref)


def _grouped_moe(hs, tile_ea, tile_eb, n_used, wts, *, tile):
    n_slots, d = hs.shape
    de = wts["w_gate"].shape[-1]
    lower = lambda j, ea, eb, nu: (ea[j], 0, 0)
    higher = lambda j, ea, eb, nu: (eb[j], 0, 0)
    w_specs = lambda idx: [pl.BlockSpec((None, d, de), idx), pl.BlockSpec((None, d, de), idx),
                           pl.BlockSpec((None, de, d), idx)]
    w_args = [wts["w_gate"], wts["w_up"], wts["w_down"]]
    return pl.pallas_call(
        _grouped_moe_kernel,
        grid_spec=pltpu.PrefetchScalarGridSpec(
            num_scalar_prefetch=3, grid=(n_slots // tile,),
            in_specs=[pl.BlockSpec((tile, d), lambda j, ea, eb, nu: (j, 0))]
                     + w_specs(lower) + w_specs(higher),
            out_specs=pl.BlockSpec((tile, 2 * d), lambda j, ea, eb, nu: (j, 0))),
        out_shape=jax.ShapeDtypeStruct((n_slots, 2 * d), F32),
        compiler_params=pltpu.CompilerParams(dimension_semantics=("arbitrary",),
                                             vmem_limit_bytes=VMEM_LIMIT_BYTES),
    )(tile_ea, tile_eb, n_used, hs, *w_args, *w_args)


def _combine_kernel(slot_ref, x1_ref, route_ref, mod_ref, modf_ref, gfin_ref, ys_hbm, y_ref,
                    ybuf, sems, *, tile, n_steps):
    t = pl.program_id(0)
    d = x1_ref.shape[-1]

    def fetch(step, buf):
        def issue(r, carry):
            pltpu.make_async_copy(ys_hbm.at[pl.ds(slot_ref[step * tile + r], 1)],
                                  ybuf.at[buf, pl.ds(r, 1)], sems.at[buf]).start()
            return carry
        lax.fori_loop(0, tile, issue, 0, unroll=8)

    @pl.when(t == 0)
    def _():
        fetch(0, 0)

    @pl.when(t + 1 < n_steps)
    def _():
        fetch(t + 1, (t + 1) % 2)

    buf = t % 2
    pltpu.make_async_copy(ys_hbm.at[pl.ds(0, tile)], ybuf.at[buf], sems.at[buf]).wait()
    yy = ybuf[buf]
    route = route_ref[...]
    moe = route[:, WA_LANE:WA_LANE + 1] * yy[:, :d] + route[:, WB_LANE:WB_LANE + 1] * yy[:, d:]
    x2 = x1_ref[...] + mod_ref[0, :, 5 * d:6 * d] * moe
    y_ref[...] = _rms_norm(x2, gfin_ref[...]) * (1.0 + modf_ref[0, :, d:2 * d]) + modf_ref[0, :, 0:d]


def _combine(slots, x1, route, mod, modf, ys, wts, *, tile, tiles_per_seq):
    n, d = x1.shape
    n_steps = n // tile
    tok = lambda t, s: (t, 0)
    per_seq = lambda t, s: (t // tiles_per_seq, 0, 0)
    return pl.pallas_call(
        functools.partial(_combine_kernel, tile=tile, n_steps=n_steps),
        grid_spec=pltpu.PrefetchScalarGridSpec(
            num_scalar_prefetch=1, grid=(n_steps,),
            in_specs=[pl.BlockSpec((tile, d), tok),
                      pl.BlockSpec((tile, ROUTER_LANES), tok),
                      pl.BlockSpec((1, 1, mod.shape[-1]), per_seq),
                      pl.BlockSpec((1, 1, modf.shape[-1]), per_seq),
                      pl.BlockSpec((1, d), lambda t, s: (0, 0)),
                      pl.BlockSpec(memory_space=pl.ANY)],
            out_specs=pl.BlockSpec((tile, d), tok),
            scratch_shapes=[pltpu.VMEM((2, tile, 2 * d), F32), pltpu.SemaphoreType.DMA((2,))]),
        out_shape=jax.ShapeDtypeStruct((n, d), F32),
        compiler_params=pltpu.CompilerParams(dimension_semantics=("arbitrary",),
                                             vmem_limit_bytes=VMEM_LIMIT_BYTES),
    )(slots, x1, route, mod, modf, wts["g_final"], ys)


def _sorted_moe(h2, x1, route, counts, mod, modf, wts, *, tiles_per_seq):
    n, d = x1.shape
    tile = SORT_TILE
    n_tiles = n // tile + N_BUCKETS
    cnt = counts[:N_BUCKETS].astype(jnp.int32)
    padded = (cnt + tile - 1) // tile * tile
    ends = jnp.cumsum(padded)
    starts = ends - padded
    n_used = ends[-1] // tile
    tile_start = jnp.arange(n_tiles, dtype=jnp.int32) * tile
    tile_bucket = jnp.sum((ends[None, :] <= tile_start[:, None]).astype(jnp.int32), axis=1)
    last_bucket = tile_bucket[n_used - 1]
    tile_bucket = jnp.where(jnp.arange(n_tiles) < n_used, tile_bucket, last_bucket)
    experts = jnp.asarray(_BUCKET_EXPERTS)[tile_bucket]
    bucket_start = jnp.zeros((ROUTER_LANES,), F32).at[:N_BUCKETS].set(starts.astype(F32))

    slots = _plan(route, bucket_start, tile=tile)
    hs = _dispatch(slots, h2, n_tiles * tile, tile=tile)
    ys = _grouped_moe(hs, experts[:, 0], experts[:, 1], n_used.reshape(1), wts, tile=tile)
    return _combine(slots, x1, route, mod, modf, ys, wts, tile=tile, tiles_per_seq=tiles_per_seq)


def _layer_weights(l,w_in, b_in, conv_w, conv_b, ln_g, ln_b, w_a, v_ln_g, v_ln_b, w_s, b_s,
                   w_b, w_o, g_norm1, g_norm2, w_rg, b_rg, w_re, b_re, w_gate, w_up, w_down,
                   g_final):
    d = w_in.shape[1]
    gd = d // GMLP_GROUPS
    zero_row = jnp.zeros((1, d), F32)
    vecs = jnp.stack([conv_b[l], ln_g[l], ln_b[l], v_ln_g[l], v_ln_b[l], g_norm1[l], g_norm2[l],
                      zero_row[0]])
    taps = jnp.concatenate([conv_w[l][::-1], jnp.zeros((CONV_TAPS_PAD - CONV_WIDTH, d), F32)])
    pad = ROUTER_LANES - N_GROUPS - N_EXPERTS
    w_r = jnp.concatenate([w_rg[l], w_re[l], jnp.zeros((d, pad), F32)], axis=1)
    b_r = jnp.concatenate([b_rg[l], b_re[l], jnp.zeros((pad,), F32)]).reshape(1, ROUTER_LANES)
    return dict(
        vecs=vecs, taps=taps,
        w_in=w_in[l].astype(BF16), b_in=b_in[l].reshape(1, -1),
        w_a=w_a[l].astype(BF16), w_b=w_b[l].astype(BF16), w_o=w_o[l].astype(BF16),
        w_s=w_s[l], b_s=jnp.repeat(b_s[l].T, gd, axis=1),
        w_r=w_r.astype(BF16), b_r=b_r,
        w_gate=w_gate[l].astype(BF16), w_up=w_up[l].astype(BF16), w_down=w_down[l].astype(BF16),
        g_final=g_final.reshape(1, d))


def kernel(x_prompt, x_sample, cache_conv, c_prompt, c_sample, w_ada, b_ada, g_norm1, g_norm2, w_in, b_in, conv_w, conv_b, ln_g, ln_b, w_a, v_ln_g, v_ln_b, w_s, b_s, w_b, w_o, w_rg, b_rg, w_re, b_re, w_gate, w_up, w_down, g_final, w_ada_f, b_ada_f):
    depth = w_in.shape[0]
    assert depth == 1, "the streaming step is written for a single layer"
    l = 0
    batch, seq, d = x_prompt.shape
    dec_batch, dec_seq, _ = x_sample.shape
    assert dec_seq == HIST_PAD and seq % MIXER_TILE == 0 and (dec_batch * dec_seq) % MIXER_TILE == 0

    wts = _layer_weights(l, w_in, b_in, conv_w, conv_b, ln_g, ln_b, w_a, v_ln_g, v_ln_b, w_s, b_s,
                         w_b, w_o, g_norm1, g_norm2, w_rg, b_rg, w_re, b_re, w_gate, w_up, w_down,
                         g_final)

    c_all = jnp.concatenate([c_prompt, c_sample], axis=0)
    mod = _modulation(c_all, w_ada[l], b_ada[l])[:, None, :]
    modf = _modulation(c_all, w_ada_f, b_ada_f)[:, None, :]

    x1p, h2p, routep, histp, counts = _mixer(x_prompt, mod[:batch], None, wts, tile=MIXER_TILE,
                                             seg=MIXER_TILE, emit_v=False, h2_dtype=F32)
    n_prompt = batch * seq
    y_prompt = _sorted_moe(h2p.reshape(n_prompt, d), x1p.reshape(n_prompt, d),
                           routep.reshape(n_prompt, ROUTER_LANES), counts[0], mod[:batch],
                           modf[:batch], wts, tiles_per_seq=seq // SORT_TILE).reshape(batch, seq, d)

    n_tok = dec_batch * dec_seq
    xs = x_sample.reshape(n_tok // MIXER_TILE, MIXER_TILE, d)
    hist_s = jnp.pad(cache_conv[l], ((0, 0), (HIST_PAD - HIST, 0), (0, 0)))
    x1s, h2s, combs, hists, _, vs = _mixer(xs, mod[batch:], hist_s, wts, tile=MIXER_TILE,
                                           seg=dec_seq, emit_v=True, h2_dtype=BF16)
    moe_tile_s = min(MOE_TILE, n_tok)
    y_sample = _moe(h2s.reshape(n_tok // moe_tile_s, moe_tile_s, d),
                    combs.reshape(n_tok // moe_tile_s, moe_tile_s, ROUTER_LANES),
                    x1s.reshape(n_tok // moe_tile_s, moe_tile_s, d),
                    mod[batch:], modf[batch:], wts, tile=moe_tile_s, seg=dec_seq)

    state_conv_prompt = histp[None, :, HIST_PAD - HIST:, :]
    state_conv_sample = hists[None, :, HIST_PAD - HIST:, :]
    state_gmlp_v_sample = vs.reshape(1, dec_batch, dec_seq, d)
    return (y_prompt, y_sample.reshape(dec_batch, dec_seq, d), state_conv_prompt,
            state_conv_sample, state_gmlp_v_sample)
```

```python
import functools

import jax
import jax.numpy as jnp
import numpy as np
from jax import lax
from jax.experimental import pallas as pl
from jax.experimental.pallas import tpu as pltpu

F32 = jnp.float32
BF16 = jnp.bfloat16

EPS = 1e-6
CONV_WIDTH = 31
HIST = CONV_WIDTH - 1
SUBLANES = 8
LANES = 128
HIST_PAD = 32
CONV_TAPS_PAD = 32
GMLP_CHUNK = 128
GMLP_GROUPS = 8
N_GROUPS = 4
EXPERTS_PER_GROUP = 8
N_EXPERTS = N_GROUPS * EXPERTS_PER_GROUP
ROUTER_LANES = LANES
EXPERT_LANE0 = N_GROUPS
NEG_BIG = -1e30
VMEM_LIMIT_BYTES = 56 * 1024 * 1024

MIXER_TILE = 256
PROMPT_TILE = 512
MOE_TILE = 1024
SORT_TILE = 256
DISPATCH_TILE = 1024

PAIRS_PER_GROUP = EXPERTS_PER_GROUP * (EXPERTS_PER_GROUP - 1) // 2
N_BUCKETS = N_GROUPS * PAIRS_PER_GROUP
BUCKET_LANE = 64
WA_LANE = 65
WB_LANE = 66
_BUCKET_EXPERTS = np.array([(g * EXPERTS_PER_GROUP + a, g * EXPERTS_PER_GROUP + b)
                            for g in range(N_GROUPS)
                            for a in range(EXPERTS_PER_GROUP)
                            for b in range(a + 1, EXPERTS_PER_GROUP)], dtype=np.int32)


def _sigmoid(x):
    return 1.0 / (1.0 + jnp.exp(-x))


def _gelu_tanh(x):
    c = 0.7978845608028654
    return 0.5 * x * (1.0 + jnp.tanh(c * (x + 0.044715 * (x * x * x))))


def _rms_norm(x, g):
    return (x * lax.rsqrt(jnp.mean(x * x, axis=-1, keepdims=True) + EPS)) * g


def _layer_norm(x, g, b):
    mu = jnp.mean(x, axis=-1, keepdims=True)
    xc = x - mu
    var = jnp.mean(xc * xc, axis=-1, keepdims=True)
    return xc * lax.rsqrt(var + EPS) * g + b


def _segment_rows(ref, lo, hi, n_seg, seg):
    if n_seg == 1:
        return ref[0, :, lo:hi]
    parts = [jnp.broadcast_to(ref[g, :, lo:hi], (seg, hi - lo)) for g in range(n_seg)]
    return jnp.concatenate(parts, axis=0)


def _modulation_kernel(c_ref, w_ref, b_ref, o_ref):
    c = c_ref[...]
    s = c * _sigmoid(c)
    o_ref[...] = jnp.dot(s, w_ref[...], precision=lax.Precision.HIGHEST,
                         preferred_element_type=F32) + b_ref[...]


def _modulation(c, w, b):
    n_seq, d = c.shape
    n_out = w.shape[1]
    bn = 1024
    return pl.pallas_call(
        _modulation_kernel,
        grid=(n_out // bn,),
        in_specs=[pl.BlockSpec((n_seq, d), lambda j: (0, 0)),
                  pl.BlockSpec((d, bn), lambda j: (0, j)),
                  pl.BlockSpec((1, bn), lambda j: (0, j))],
        out_specs=pl.BlockSpec((n_seq, bn), lambda j: (0, j)),
        out_shape=jax.ShapeDtypeStruct((n_seq, n_out), F32),
    )(c, w, b.reshape(1, n_out))


def _causal_conv(a_ext, taps, seg):
    rows = seg + SUBLANES
    n_q = CONV_TAPS_PAD // SUBLANES
    shifted = [a_ext[HIST_PAD - SUBLANES * (q + 1): HIST_PAD - SUBLANES * (q + 1) + rows]
               for q in range(n_q)]
    acc = None
    for r in reversed(range(SUBLANES)):
        b_r = None
        for q in range(n_q):
            j = SUBLANES * q + r
            if j >= CONV_WIDTH:
                continue
            term = taps[j:j + 1] * shifted[q]
            b_r = term if b_r is None else b_r + term
        acc = b_r if acc is None else b_r + pltpu.roll(acc, 1, axis=0)
    return acc[SUBLANES:]


def _mixer_kernel(*refs, tile, sub, seg, has_hist, emit_v):
    it = iter(refs)
    x_ref = next(it)
    mod_ref = next(it)
    hist_ref = next(it) if has_hist else None
    vecs_ref = next(it)
    w_in_ref = next(it)
    b_in_ref = next(it)
    taps_ref = next(it)
    w_a_ref = next(it)
    ws_ref = next(it)
    bs_ref = next(it)
    w_b_ref = next(it)
    w_o_ref = next(it)
    w_r_ref = next(it)
    b_r_ref = next(it)
    x1_ref = next(it)
    h2_ref = next(it)
    route_ref = next(it)
    hist_out_ref = next(it)
    count_ref = next(it)
    v_out_ref = next(it) if emit_v else None
    count_acc_ref = next(it)
    carry_ref = None if has_hist else next(it)

    if not has_hist:
        @pl.when(pl.program_id(1) == 0)
        def _():
            carry_ref[...] = jnp.zeros_like(carry_ref)

    @pl.when((pl.program_id(0) == 0) & (pl.program_id(1) == 0))
    def _():
        count_acc_ref[...] = jnp.zeros_like(count_acc_ref)

    for k in range(tile // sub):
        rows = pl.ds(k * sub, sub)
        _mixer_rows(x_ref.at[rows], mod_ref, hist_ref, vecs_ref, w_in_ref, b_in_ref, taps_ref,
                    w_a_ref, ws_ref, bs_ref, w_b_ref, w_o_ref, w_r_ref, b_r_ref,
                    x1_ref.at[rows], h2_ref.at[rows], route_ref.at[rows], hist_out_ref,
                    v_out_ref.at[rows] if emit_v else None, count_acc_ref, carry_ref,
                    tile=sub, seg=min(seg, sub))
    count_ref[...] = count_acc_ref[...]


def _mixer_rows(x_ref, mod_ref, hist_ref, vecs_ref, w_in_ref, b_in_ref, taps_ref, w_a_ref, ws_ref,
                bs_ref, w_b_ref, w_o_ref, w_r_ref, b_r_ref, x1_ref, h2_ref, route_ref, hist_out_ref,
                v_out_ref, count_acc_ref, carry_ref, *, tile, seg):
    has_hist = hist_ref is not None
    emit_v = v_out_ref is not None
    d = x_ref.shape[-1]
    n_seg = tile // seg
    chunk = min(seg, GMLP_CHUNK)
    n_chunk = tile // chunk

    vecs = vecs_ref[...]
    conv_b, ln_g, ln_b, v_ln_g, v_ln_b, g_norm1, g_norm2 = [vecs[i:i + 1] for i in range(7)]
    mod = lambda k: _segment_rows(mod_ref, k * d, (k + 1) * d, n_seg, seg)

    x = x_ref[...]
    h = _rms_norm(x, g_norm1) * (1.0 + mod(1)) + mod(0)
    hb = h.astype(BF16)

    za = jnp.dot(hb, w_in_ref[:, 0:2 * d], preferred_element_type=F32) + b_in_ref[:, 0:2 * d]
    a = za[:, :d] * _sigmoid(za[:, d:])
    taps = taps_ref[...]
    conv_parts = []
    for g in range(n_seg):
        a_seg = a[g * seg:(g + 1) * seg]
        hist = hist_ref[g] if has_hist else carry_ref[...]
        conv_parts.append(_causal_conv(jnp.concatenate([hist, a_seg], axis=0), taps, seg))
        hist_out_ref[g] = a_seg[seg - HIST_PAD:]
    if not has_hist:
        carry_ref[...] = a[tile - HIST_PAD:]
    conv = conv_parts[0] if n_seg == 1 else jnp.concatenate(conv_parts, axis=0)
    ca = _layer_norm(conv + conv_b, ln_g, ln_b)
    ca = ca * _sigmoid(ca)
    y_a = jnp.dot(ca.astype(BF16), w_a_ref[...], preferred_element_type=F32)

    zuv = jnp.dot(hb, w_in_ref[:, 2 * d:4 * d], preferred_element_type=F32) + b_in_ref[:, 2 * d:4 * d]
    guv = _gelu_tanh(zuv)
    u = guv[:, :d]
    v = _layer_norm(guv[:, d:], v_ln_g, v_ln_b)
    if emit_v:
        v_out_ref[...] = v
    vb = v.astype(BF16)
    gd = d // GMLP_GROUPS
    causal = (lax.broadcasted_iota(jnp.int32, (chunk, chunk), 0)
              >= lax.broadcasted_iota(jnp.int32, (chunk, chunk), 1))
    mixed_cols = []
    for hg in range(GMLP_GROUPS):
        w = jnp.where(causal, ws_ref[hg], 0.0).astype(BF16)
        rhs = [vb[c * chunk:(c + 1) * chunk, hg * gd:(hg + 1) * gd] for c in range(n_chunk)]
        rhs = rhs[0] if n_chunk == 1 else jnp.concatenate(rhs, axis=1)
        out = jnp.dot(w, rhs, preferred_element_type=F32)
        cols = [out[:, c * gd:(c + 1) * gd] for c in range(n_chunk)]
        mixed_cols.append(cols[0] if n_chunk == 1 else jnp.concatenate(cols, axis=0))
    mixed = jnp.concatenate(mixed_cols, axis=1)
    bs = bs_ref[...]
    mixed = mixed + (bs if n_chunk == 1 else jnp.concatenate([bs] * n_chunk, axis=0))
    y_b = jnp.dot((u * mixed).astype(BF16), w_b_ref[...], preferred_element_type=F32)

    zg = jnp.dot(hb, w_in_ref[:, 4 * d:6 * d], preferred_element_type=F32) + b_in_ref[:, 4 * d:6 * d]
    gates = _sigmoid(zg)
    merged = gates[:, :d] * y_a + gates[:, d:] * y_b
    m = jnp.dot(merged.astype(BF16), w_o_ref[...], preferred_element_type=F32)
    x1 = x + mod(2) * m
    x1_ref[...] = x1

    h2 = _rms_norm(x1, g_norm2) * (1.0 + mod(4)) + mod(3)
    h2_ref[...] = h2.astype(h2_ref.dtype)
    logits = jnp.dot(h2.astype(BF16), w_r_ref[...], preferred_element_type=F32) + b_r_ref[...]
    lane = lax.broadcasted_iota(jnp.int32, (tile, ROUTER_LANES), 1).astype(F32)
    is_group = lane < N_GROUPS
    lg = jnp.where(is_group, logits, NEG_BIG)
    mg = jnp.max(lg, axis=1, keepdims=True)
    gsel = jnp.min(jnp.where(lg == mg, lane, float(ROUTER_LANES)), axis=1, keepdims=True)
    pgsel = 1.0 / jnp.sum(jnp.where(is_group, jnp.exp(lg - mg), 0.0), axis=1, keepdims=True)
    lo = EXPERT_LANE0 + EXPERTS_PER_GROUP * gsel
    le = jnp.where((lane >= lo) & (lane < lo + EXPERTS_PER_GROUP), logits, NEG_BIG)
    m1 = jnp.max(le, axis=1, keepdims=True)
    i1 = jnp.min(jnp.where(le == m1, lane, float(ROUTER_LANES)), axis=1, keepdims=True)
    le2 = jnp.where(lane == i1, NEG_BIG, le)
    m2 = jnp.max(le2, axis=1, keepdims=True)
    i2 = jnp.min(jnp.where(le2 == m2, lane, float(ROUTER_LANES)), axis=1, keepdims=True)
    e2 = jnp.exp(m2 - m1)
    w1 = pgsel / (1.0 + e2)
    w2 = w1 * e2
    first_lower = i1 < i2
    ea = jnp.where(first_lower, i1, i2) - lo
    eb = jnp.where(first_lower, i2, i1) - lo
    bucket = gsel * PAIRS_PER_GROUP + ea * (2 * EXPERTS_PER_GROUP - 1 - ea) * 0.5 + (eb - ea - 1.0)
    wa = jnp.where(first_lower, w1, w2)
    wb = jnp.where(first_lower, w2, w1)
    route_ref[...] = (jnp.where(lane == i1, w1, 0.0) + jnp.where(lane == i2, w2, 0.0)
                      + jnp.where(lane == BUCKET_LANE, bucket, 0.0)
                      + jnp.where(lane == WA_LANE, wa, 0.0) + jnp.where(lane == WB_LANE, wb, 0.0))

    count_acc_ref[...] += jnp.sum(jnp.where(lane == bucket, 1.0, 0.0), axis=0, keepdims=True)


def _const_spec(shape):
    nd = len(shape)
    return pl.BlockSpec(shape, lambda i, t: (0,) * nd, pipeline_mode=pl.Buffered(1))


def _mixer(x, mod, hist, wts, *, tile, sub, seg, emit_v, h2_dtype):
    nb, s, d = x.shape
    n_seg = tile // seg
    has_hist = hist is not None
    grid = (nb, s // tile)
    chunk = min(seg, GMLP_CHUNK)

    in_specs = [pl.BlockSpec((None, tile, d), lambda i, t: (i, t, 0)),
                pl.BlockSpec((n_seg, 1, mod.shape[-1]), lambda i, t: (i, 0, 0))]
    args = [x, mod]
    if has_hist:
        in_specs.append(pl.BlockSpec((n_seg, HIST_PAD, d), lambda i, t: (i, 0, 0)))
        args.append(hist)
    consts = [wts["vecs"], wts["w_in"], wts["b_in"], wts["taps"], wts["w_a"],
              wts["w_s"][:, :chunk, :chunk], wts["b_s"][:chunk], wts["w_b"], wts["w_o"],
              wts["w_r"], wts["b_r"]]
    in_specs += [_const_spec(c.shape) for c in consts]
    args += consts

    out_shape = [jax.ShapeDtypeStruct((nb, s, d), F32),
                 jax.ShapeDtypeStruct((nb, s, d), h2_dtype),
                 jax.ShapeDtypeStruct((nb, s, ROUTER_LANES), F32),
                 jax.ShapeDtypeStruct((nb * n_seg, HIST_PAD, d), F32),
                 jax.ShapeDtypeStruct((SUBLANES, ROUTER_LANES), F32)]
    out_specs = [pl.BlockSpec((None, tile, d), lambda i, t: (i, t, 0)),
                 pl.BlockSpec((None, tile, d), lambda i, t: (i, t, 0)),
                 pl.BlockSpec((None, tile, ROUTER_LANES), lambda i, t: (i, t, 0)),
                 pl.BlockSpec((n_seg, HIST_PAD, d), lambda i, t: (i, 0, 0)),
                 pl.BlockSpec((SUBLANES, ROUTER_LANES), lambda i, t: (0, 0))]
    if emit_v:
        out_shape.append(jax.ShapeDtypeStruct((nb, s, d), F32))
        out_specs.append(pl.BlockSpec((None, tile, d), lambda i, t: (i, t, 0)))
    scratch = [pltpu.VMEM((SUBLANES, ROUTER_LANES), F32)]
    if not has_hist:
        scratch.append(pltpu.VMEM((HIST_PAD, d), F32))

    return pl.pallas_call(
        functools.partial(_mixer_kernel, tile=tile, sub=sub, seg=seg, has_hist=has_hist,
                          emit_v=emit_v),
        grid=grid, in_specs=in_specs, out_specs=out_specs, out_shape=out_shape,
        scratch_shapes=scratch,
        compiler_params=pltpu.CompilerParams(
            dimension_semantics=("arbitrary", "arbitrary"),
            vmem_limit_bytes=VMEM_LIMIT_BYTES),
    )(*args)


def _moe_kernel(h2_ref, comb_ref, x1_ref, mod_ref, modf_ref, gfin_ref, wg_ref, wu_ref, wd_ref,
                y_ref, acc_ref, *, tile, seg):
    e = pl.program_id(2)
    d = x1_ref.shape[-1]
    n_seg = tile // seg

    @pl.when(e == 0)
    def _():
        acc_ref[...] = jnp.zeros_like(acc_ref)

    hb = h2_ref[...]
    hg = jnp.dot(hb, wg_ref[...], preferred_element_type=F32)
    hu = jnp.dot(hb, wu_ref[...], preferred_element_type=F32)
    he = (hg * _sigmoid(hg)) * hu
    ye = jnp.dot(he.astype(BF16), wd_ref[...], preferred_element_type=F32)
    lane = lax.broadcasted_iota(jnp.int32, (tile, ROUTER_LANES), 1)
    cw = jnp.sum(jnp.where(lane == e + EXPERT_LANE0, comb_ref[...], 0.0), axis=1, keepdims=True)
    acc_ref[...] += cw * ye

    @pl.when(e == N_EXPERTS - 1)
    def _():
        g2 = _segment_rows(mod_ref, 5 * d, 6 * d, n_seg, seg)
        shf = _segment_rows(modf_ref, 0, d, n_seg, seg)
        scf = _segment_rows(modf_ref, d, 2 * d, n_seg, seg)
        x2 = x1_ref[...] + g2 * acc_ref[...]
        y_ref[...] = _rms_norm(x2, gfin_ref[...]) * (1.0 + scf) + shf


def _moe(h2, comb, x1, mod, modf, wts, *, tile, seg):
    nb, s, d = x1.shape
    n_seg = tile // seg
    de = wts["w_gate"].shape[-1]
    tok = lambda i, t, e: (i, t, 0)
    per_seq = lambda i, t, e: (i, 0, 0)
    return pl.pallas_call(
        functools.partial(_moe_kernel, tile=tile, seg=seg),
        grid=(nb, s // tile, N_EXPERTS),
        in_specs=[pl.BlockSpec((None, tile, d), tok),
                  pl.BlockSpec((None, tile, ROUTER_LANES), tok),
                  pl.BlockSpec((None, tile, d), tok),
                  pl.BlockSpec((n_seg, 1, mod.shape[-1]), per_seq),
                  pl.BlockSpec((n_seg, 1, modf.shape[-1]), per_seq),
                  pl.BlockSpec((1, d), lambda i, t, e: (0, 0)),
                  pl.BlockSpec((None, d, de), lambda i, t, e: (e, 0, 0)),
                  pl.BlockSpec((None, d, de), lambda i, t, e: (e, 0, 0)),
                  pl.BlockSpec((None, de, d), lambda i, t, e: (e, 0, 0))],
        out_specs=pl.BlockSpec((None, tile, d), tok),
        out_shape=jax.ShapeDtypeStruct((nb, s, d), F32),
        scratch_shapes=[pltpu.VMEM((tile, d), F32)],
        compiler_params=pltpu.CompilerParams(
            dimension_semantics=("arbitrary", "arbitrary", "arbitrary"),
            vmem_limit_bytes=VMEM_LIMIT_BYTES),
    )(h2, comb, x1, mod, modf, wts["g_final"], wts["w_gate"], wts["w_up"], wts["w_down"])


def _plan_kernel(route_ref, start_ref, slot_ref, next_ref, *, tile):
    @pl.when(pl.program_id(0) == 0)
    def _():
        next_ref[...] = start_ref[...]

    bucket_row = route_ref[...].T[BUCKET_LANE:BUCKET_LANE + 1, :]
    rows = lax.broadcasted_iota(jnp.int32, (ROUTER_LANES, tile), 0).astype(F32)
    member = rows == bucket_row
    upper = (lax.broadcasted_iota(jnp.int32, (tile, tile), 0)
             <= lax.broadcasted_iota(jnp.int32, (tile, tile), 1))
    seen = jnp.dot(jnp.where(member, 1.0, 0.0).astype(BF16), jnp.where(upper, 1.0, 0.0).astype(BF16),
                   preferred_element_type=F32)
    nxt = next_ref[...]
    slot = jnp.sum(jnp.where(member, seen - 1.0 + nxt[:, 0:1], 0.0), axis=0, keepdims=True)
    slot_ref[...] = slot.astype(jnp.int32)
    next_ref[...] = nxt + seen[:, tile - 1:tile]


def _plan(route, bucket_start, *, tile):
    n = route.shape[0]
    start = jnp.broadcast_to(bucket_start[:, None], (ROUTER_LANES, ROUTER_LANES))
    slots = pl.pallas_call(
        functools.partial(_plan_kernel, tile=tile),
        grid=(n // tile,),
        in_specs=[pl.BlockSpec((tile, ROUTER_LANES), lambda t: (t, 0)),
                  pl.BlockSpec((ROUTER_LANES, ROUTER_LANES), lambda t: (0, 0))],
        out_specs=pl.BlockSpec((None, 1, tile), lambda t: (t, 0, 0)),
        out_shape=jax.ShapeDtypeStruct((n // tile, 1, tile), jnp.int32),
        scratch_shapes=[pltpu.VMEM((ROUTER_LANES, ROUTER_LANES), F32)],
        compiler_params=pltpu.CompilerParams(dimension_semantics=("arbitrary",)),
    )(route, start)
    return slots.reshape(n)


def _dispatch_kernel(slot_ref, pad_tile_ref, src_ref, dst_hbm, zeros_ref, sem, *, tile, pad_rows):
    t = pl.program_id(0)

    @pl.when(t == 0)
    def _():
        zeros_ref[...] = jnp.zeros_like(zeros_ref)
        for phase in ("start", "wait"):
            for b in range(2 * N_BUCKETS):
                @pl.when(pad_tile_ref[b] >= 0)
                def _():
                    first_row = pl.multiple_of(pad_tile_ref[b], pad_rows)
                    copy = pltpu.make_async_copy(
                        zeros_ref, dst_hbm.at[pl.ds(first_row, pad_rows)], sem)
                    copy.start() if phase == "start" else copy.wait()

    base = t * tile
    for r in range(tile):
        pltpu.make_async_copy(src_ref.at[pl.ds(r, 1)], dst_hbm.at[pl.ds(slot_ref[base + r], 1)],
                              sem).start()
    pltpu.make_async_copy(src_ref, dst_hbm.at[pl.ds(0, tile)], sem).wait()


def _dispatch(slots, pad_tile, src, n_slots, *, tile, pad_rows):
    n, d = src.shape
    return pl.pallas_call(
        functools.partial(_dispatch_kernel, tile=tile, pad_rows=pad_rows),
        grid_spec=pltpu.PrefetchScalarGridSpec(
            num_scalar_prefetch=2, grid=(n // tile,),
            in_specs=[pl.BlockSpec((tile, d), lambda t, s, p: (t, 0))],
            out_specs=pl.BlockSpec(memory_space=pl.ANY),
            scratch_shapes=[pltpu.VMEM((pad_rows, d), src.dtype), pltpu.SemaphoreType.DMA(())]),
        out_shape=jax.ShapeDtypeStruct((n_slots, d), src.dtype),
        compiler_params=pltpu.CompilerParams(dimension_semantics=("arbitrary",),
                                             vmem_limit_bytes=VMEM_LIMIT_BYTES),
    )(slots, pad_tile, src)


def _grouped_moe_kernel(ea_ref, eb_ref, n_used_ref, hs_ref, wga_ref, wua_ref, wda_ref,
                        wgb_ref, wub_ref, wdb_ref, ys_ref):
    del ea_ref, eb_ref
    d = hs_ref.shape[-1]
    j = pl.program_id(0)

    @pl.when(j < n_used_ref[0])
    def _():
        xb = hs_ref[...].astype(BF16)
        for k, (wg, wu, wd) in enumerate(((wga_ref, wua_ref, wda_ref), (wgb_ref, wub_ref, wdb_ref))):
            g = jnp.dot(xb, wg[...], preferred_element_type=F32)
            u = jnp.dot(xb, wu[...], preferred_element_type=F32)
            he = (g * _sigmoid(g)) * u
            ys_ref[:, k * d:(k + 1) * d] = jnp.dot(he.astype(BF16), wd[...], preferred_element_type=F32)

    @pl.when(j >= n_used_ref[0])
    def _():
        ys_ref[...] = jnp.zeros_like(ys_ref)


def _grouped_moe(hs, tile_ea, tile_eb, n_used, wts, *, tile):
    n_slots, d = hs.shape
    de = wts["w_gate"].shape[-1]
    lower = lambda j, ea, eb, nu: (ea[j], 0, 0)
    higher = lambda j, ea, eb, nu: (eb[j], 0, 0)
    w_specs = lambda idx: [pl.BlockSpec((None, d, de), idx), pl.BlockSpec((None, d, de), idx),
                           pl.BlockSpec((None, de, d), idx)]
    w_args = [wts["w_gate"], wts["w_up"], wts["w_down"]]
    return pl.pallas_call(
        _grouped_moe_kernel,
        grid_spec=pltpu.PrefetchScalarGridSpec(
            num_scalar_prefetch=3, grid=(n_slots // tile,),
            in_specs=[pl.BlockSpec((tile, d), lambda j, ea, eb, nu: (jnp.minimum(j, nu[0] - 1), 0))]
                     + w_specs(lower) + w_specs(higher),
            out_specs=pl.BlockSpec((tile, 2 * d), lambda j, ea, eb, nu: (j, 0))),
        out_shape=jax.ShapeDtypeStruct((n_slots, 2 * d), F32),
        compiler_params=pltpu.CompilerParams(dimension_semantics=("arbitrary",),
                                             vmem_limit_bytes=VMEM_LIMIT_BYTES),
    )(tile_ea, tile_eb, n_used, hs, *w_args, *w_args)


def _combine_kernel(slot_ref, x1_ref, route_ref, mod_ref, modf_ref, gfin_ref, ys_hbm, y_ref,
                    ybuf, sems, *, tile, n_steps):
    t = pl.program_id(0)
    d = x1_ref.shape[-1]

    def fetch(step, buf):
        for r in range(tile):
            pltpu.make_async_copy(ys_hbm.at[pl.ds(slot_ref[step * tile + r], 1)],
                                  ybuf.at[buf, pl.ds(r, 1)], sems.at[buf]).start()

    def wait(buf):
        pltpu.make_async_copy(ys_hbm.at[pl.ds(0, tile)], ybuf.at[buf], sems.at[buf]).wait()

    @pl.when(t == 0)
    def _():
        fetch(0, 0)

    def step(buf):
        fetch((t + 1) % n_steps, 1 - buf)
        wait(buf)
        yy = ybuf[buf]
        route = route_ref[...]
        moe = route[:, WA_LANE:WA_LANE + 1] * yy[:, :d] + route[:, WB_LANE:WB_LANE + 1] * yy[:, d:]
        x2 = x1_ref[...] + mod_ref[0, :, 5 * d:6 * d] * moe
        y_ref[...] = (_rms_norm(x2, gfin_ref[...]) * (1.0 + modf_ref[0, :, d:2 * d])
                      + modf_ref[0, :, 0:d])

    for buf in range(2):
        pl.when(t % 2 == buf)(functools.partial(step, buf))

    @pl.when(t == n_steps - 1)
    def _():
        wait(n_steps % 2)


def _combine(slots, x1, route, mod, modf, ys, wts, *, tile, tiles_per_seq):
    n, d = x1.shape
    n_steps = n // tile
    tok = lambda t, s: (t, 0)
    per_seq = lambda t, s: (t // tiles_per_seq, 0, 0)
    return pl.pallas_call(
        functools.partial(_combine_kernel, tile=tile, n_steps=n_steps),
        grid_spec=pltpu.PrefetchScalarGridSpec(
            num_scalar_prefetch=1, grid=(n_steps,),
            in_specs=[pl.BlockSpec((tile, d), tok),
                      pl.BlockSpec((tile, ROUTER_LANES), tok),
                      pl.BlockSpec((1, 1, mod.shape[-1]), per_seq),
                      pl.BlockSpec((1, 1, modf.shape[-1]), per_seq),
                      pl.BlockSpec((1, d), lambda t, s: (0, 0)),
                      pl.BlockSpec(memory_space=pl.ANY)],
            out_specs=pl.BlockSpec((tile, d), tok),
            scratch_shapes=[pltpu.VMEM((2, tile, 2 * d), F32), pltpu.SemaphoreType.DMA((2,))]),
        out_shape=jax.ShapeDtypeStruct((n, d), F32),
        compiler_params=pltpu.CompilerParams(dimension_semantics=("arbitrary",),
                                             vmem_limit_bytes=VMEM_LIMIT_BYTES),
    )(slots, x1, route, mod, modf, wts["g_final"], ys)


def _sorted_moe(h2, x1, route, counts, mod, modf, wts, *, tiles_per_seq):
    n, d = x1.shape
    tile = SORT_TILE
    n_tiles = n // tile + N_BUCKETS
    cnt = counts[:N_BUCKETS].astype(jnp.int32)
    padded = (cnt + tile - 1) // tile * tile
    ends = jnp.cumsum(padded)
    starts = ends - padded
    n_used = ends[-1] // tile
    tile_start = jnp.arange(n_tiles, dtype=jnp.int32) * tile
    tile_bucket = jnp.sum((ends[None, :] <= tile_start[:, None]).astype(jnp.int32), axis=1)
    last_bucket = tile_bucket[n_used - 1]
    tile_bucket = jnp.where(jnp.arange(n_tiles) < n_used, tile_bucket, last_bucket)
    experts = jnp.asarray(_BUCKET_EXPERTS)[tile_bucket]
    bucket_start = jnp.zeros((ROUTER_LANES,), F32).at[:N_BUCKETS].set(starts.astype(F32))

    tail_tile = n_used + jnp.arange(N_BUCKETS, dtype=jnp.int32)
    pad_tile = jnp.concatenate([jnp.where(padded > cnt, ends - tile, -1),
                                jnp.where(tail_tile < n_tiles, tail_tile * tile, -1)])

    slots = _plan(route, bucket_start, tile=DISPATCH_TILE)
    hs = _dispatch(slots, pad_tile, h2, n_tiles * tile, tile=DISPATCH_TILE, pad_rows=tile)
    ys = _grouped_moe(hs, experts[:, 0], experts[:, 1], n_used.reshape(1), wts, tile=tile)
    return _combine(slots, x1, route, mod, modf, ys, wts, tile=tile, tiles_per_seq=tiles_per_seq)


def _layer_weights(l,w_in, b_in, conv_w, conv_b, ln_g, ln_b, w_a, v_ln_g, v_ln_b, w_s, b_s,
                   w_b, w_o, g_norm1, g_norm2, w_rg, b_rg, w_re, b_re, w_gate, w_up, w_down,
                   g_final):
    d = w_in.shape[1]
    gd = d // GMLP_GROUPS
    zero_row = jnp.zeros((1, d), F32)
    vecs = jnp.stack([conv_b[l], ln_g[l], ln_b[l], v_ln_g[l], v_ln_b[l], g_norm1[l], g_norm2[l],
                      zero_row[0]])
    taps = jnp.concatenate([conv_w[l][::-1], jnp.zeros((CONV_TAPS_PAD - CONV_WIDTH, d), F32)])
    pad = ROUTER_LANES - N_GROUPS - N_EXPERTS
    w_r = jnp.concatenate([w_rg[l], w_re[l], jnp.zeros((d, pad), F32)], axis=1)
    b_r = jnp.concatenate([b_rg[l], b_re[l], jnp.zeros((pad,), F32)]).reshape(1, ROUTER_LANES)
    return dict(
        vecs=vecs, taps=taps,
        w_in=w_in[l].astype(BF16), b_in=b_in[l].reshape(1, -1),
        w_a=w_a[l].astype(BF16), w_b=w_b[l].astype(BF16), w_o=w_o[l].astype(BF16),
        w_s=w_s[l], b_s=jnp.repeat(b_s[l].T, gd, axis=1),
        w_r=w_r.astype(BF16), b_r=b_r,
        w_gate=w_gate[l].astype(BF16), w_up=w_up[l].astype(BF16), w_down=w_down[l].astype(BF16),
        g_final=g_final.reshape(1, d))


def kernel(x_prompt, x_sample, cache_conv, c_prompt, c_sample, w_ada, b_ada, g_norm1, g_norm2, w_in, b_in, conv_w, conv_b, ln_g, ln_b, w_a, v_ln_g, v_ln_b, w_s, b_s, w_b, w_o, w_rg, b_rg, w_re, b_re, w_gate, w_up, w_down, g_final, w_ada_f, b_ada_f):
    depth = w_in.shape[0]
    assert depth == 1, "the streaming step is written for a single layer"
    l = 0
    batch, seq, d = x_prompt.shape
    dec_batch, dec_seq, _ = x_sample.shape
    assert dec_seq == HIST_PAD and seq % PROMPT_TILE == 0 and (dec_batch * dec_seq) % MIXER_TILE == 0
    assert (batch * seq) % DISPATCH_TILE == 0

    wts = _layer_weights(l, w_in, b_in, conv_w, conv_b, ln_g, ln_b, w_a, v_ln_g, v_ln_b, w_s, b_s,
                         w_b, w_o, g_norm1, g_norm2, w_rg, b_rg, w_re, b_re, w_gate, w_up, w_down,
                         g_final)

    c_all = jnp.concatenate([c_prompt, c_sample], axis=0)
    mod = _modulation(c_all, w_ada[l], b_ada[l])[:, None, :]
    modf = _modulation(c_all, w_ada_f, b_ada_f)[:, None, :]

    x1p, h2p, routep, histp, counts = _mixer(x_prompt, mod[:batch], None, wts, tile=PROMPT_TILE,
                                             sub=MIXER_TILE, seg=PROMPT_TILE, emit_v=False,
                                             h2_dtype=F32)
    n_prompt = batch * seq
    y_prompt = _sorted_moe(h2p.reshape(n_prompt, d), x1p.reshape(n_prompt, d),
                           routep.reshape(n_prompt, ROUTER_LANES), counts[0], mod[:batch],
                           modf[:batch], wts, tiles_per_seq=seq // SORT_TILE).reshape(batch, seq, d)

    n_tok = dec_batch * dec_seq
    xs = x_sample.reshape(n_tok // MIXER_TILE, MIXER_TILE, d)
    hist_s = jnp.pad(cache_conv[l], ((0, 0), (HIST_PAD - HIST, 0), (0, 0)))
    x1s, h2s, combs, hists, _, vs = _mixer(xs, mod[batch:], hist_s, wts, tile=MIXER_TILE,
                                           sub=MIXER_TILE, seg=dec_seq, emit_v=True, h2_dtype=BF16)
    moe_tile_s = min(MOE_TILE, n_tok)
    y_sample = _moe(h2s.reshape(n_tok // moe_tile_s, moe_tile_s, d),
                    combs.reshape(n_tok // moe_tile_s, moe_tile_s, ROUTER_LANES),
                    x1s.reshape(n_tok // moe_tile_s, moe_tile_s, d),
                    mod[batch:], modf[batch:], wts, tile=moe_tile_s, seg=dec_seq)

    state_conv_prompt = histp[None, :, HIST_PAD - HIST:, :]
    state_conv_sample = hists[None, :, HIST_PAD - HIST:, :]
    state_gmlp_v_sample = vs.reshape(1, dec_batch, dec_seq, d)
    return (y_prompt, y_sample.reshape(dec_batch, dec_seq, d), state_conv_prompt,
            state_conv_sample, state_gmlp_v_sample)
```

```python
import functools

import jax
import jax.numpy as jnp
import numpy as np
from jax import lax
from jax.experimental import pallas as pl
from jax.experimental.pallas import tpu as pltpu

F32 = jnp.float32
BF16 = jnp.bfloat16

EPS = 1e-6
CONV_WIDTH = 31
HIST = CONV_WIDTH - 1
SUBLANES = 8
LANES = 128
HIST_PAD = 32
CONV_TAPS_PAD = 32
GMLP_CHUNK = 128
GMLP_GROUPS = 8
N_GROUPS = 4
EXPERTS_PER_GROUP = 8
N_EXPERTS = N_GROUPS * EXPERTS_PER_GROUP
ROUTER_LANES = LANES
EXPERT_LANE0 = N_GROUPS
NEG_BIG = -1e30
VMEM_LIMIT_BYTES = 56 * 1024 * 1024

MIXER_TILE = 256
PROMPT_TILE = 512
CONV_COL_BLOCKS = 4
MIXER_SKEW = 1
MOE_TILE = 1024
SORT_TILE = 256
DISPATCH_TILE = 1024

PAIRS_PER_GROUP = EXPERTS_PER_GROUP * (EXPERTS_PER_GROUP - 1) // 2
N_BUCKETS = N_GROUPS * PAIRS_PER_GROUP
BUCKET_LANE = 64
WA_LANE = 65
WB_LANE = 66
_BUCKET_EXPERTS = np.array([(g * EXPERTS_PER_GROUP + a, g * EXPERTS_PER_GROUP + b)
                            for g in range(N_GROUPS)
                            for a in range(EXPERTS_PER_GROUP)
                            for b in range(a + 1, EXPERTS_PER_GROUP)], dtype=np.int32)


def _sigmoid(x):
    return 1.0 / (1.0 + jnp.exp2(x * -1.4426950408889634))


def _gelu_tanh(x):
    c = 0.7978845608028654
    inner = x * (c + (c * 0.044715) * (x * x))
    return (0.5 * x) * (1.0 + jnp.tanh(inner))


def _rms_norm(x, g):
    return (x * lax.rsqrt(jnp.mean(x * x, axis=-1, keepdims=True) + EPS)) * g


def _layer_norm(x, g, b):
    mu = jnp.mean(x, axis=-1, keepdims=True)
    xc = x - mu
    var = jnp.mean(xc * xc, axis=-1, keepdims=True)
    return xc * lax.rsqrt(var + EPS) * g + b


def _segment_rows(ref, lo, hi, n_seg, seg):
    if n_seg == 1:
        return ref[0, :, lo:hi]
    parts = [jnp.broadcast_to(ref[g, :, lo:hi], (seg, hi - lo)) for g in range(n_seg)]
    return jnp.concatenate(parts, axis=0)


def _modulation_kernel(c_ref, w_ref, b_ref, o_ref):
    c = c_ref[...]
    s = c * _sigmoid(c)
    o_ref[...] = jnp.dot(s, w_ref[...], precision=lax.Precision.HIGHEST,
                         preferred_element_type=F32) + b_ref[...]


def _modulation(c, w, b):
    n_seq, d = c.shape
    n_out = w.shape[1]
    bn = 1024
    return pl.pallas_call(
        _modulation_kernel,
        grid=(n_out // bn,),
        in_specs=[pl.BlockSpec((n_seq, d), lambda j: (0, 0)),
                  pl.BlockSpec((d, bn), lambda j: (0, j)),
                  pl.BlockSpec((1, bn), lambda j: (0, j))],
        out_specs=pl.BlockSpec((n_seq, bn), lambda j: (0, j)),
        out_shape=jax.ShapeDtypeStruct((n_seq, n_out), F32),
    )(c, w, b.reshape(1, n_out))


def _causal_conv(a_ext, taps_ref, cols, seg):
    n_q = CONV_TAPS_PAD // SUBLANES
    n_hist = HIST_PAD // SUBLANES
    group = lambda m: a_ext[SUBLANES * m:SUBLANES * (m + 1), cols]
    width = group(0).shape[-1]
    last_row = lax.broadcasted_iota(jnp.int32, (SUBLANES, width), 0) == SUBLANES - 1
    prev = [jnp.zeros((SUBLANES, width), F32)] * SUBLANES
    out = []
    for k in range(-1, seg // SUBLANES):
        operands = [group(n_hist + k - q) for q in range(n_q)]
        partial = [None] * SUBLANES
        for r in reversed(range(SUBLANES)):
            b_r = None
            for q in range(n_q):
                j = SUBLANES * q + r
                if j >= CONV_WIDTH:
                    continue
                term = taps_ref[j, :, cols] * operands[q]
                b_r = term if b_r is None else b_r + term
            if r < SUBLANES - 1:
                b_r = b_r + pltpu.roll(jnp.where(last_row, prev[r + 1], partial[r + 1]), 1, axis=0)
            partial[r] = b_r
        prev = partial
        if k >= 0:
            out.append(partial[0])
    return jnp.concatenate(out, axis=0)


def _mixer_kernel(*refs, tile, sub, seg, has_hist, emit_v):
    it = iter(refs)
    x_ref = next(it)
    mod_ref = next(it)
    hist_ref = next(it) if has_hist else None
    vecs_ref = next(it)
    w_in_ref = next(it)
    b_in_ref = next(it)
    taps_ref = next(it)
    w_a_ref = next(it)
    ws_ref = next(it)
    bs_ref = next(it)
    w_b_ref = next(it)
    w_o_ref = next(it)
    w_r_ref = next(it)
    b_r_ref = next(it)
    x1_ref = next(it)
    h2_ref = next(it)
    route_ref = next(it)
    hist_out_ref = next(it)
    count_ref = next(it)
    v_out_ref = next(it) if emit_v else None
    count_acc_ref = next(it)
    carry_ref = None if has_hist else next(it)

    if not has_hist:
        @pl.when(pl.program_id(1) == 0)
        def _():
            carry_ref[...] = jnp.zeros_like(carry_ref)

    @pl.when((pl.program_id(0) == 0) & (pl.program_id(1) == 0))
    def _():
        count_acc_ref[...] = jnp.zeros_like(count_acc_ref)

    windows = []
    for k in range(tile // sub):
        rows = pl.ds(k * sub, sub)
        windows.append(_mixer_stages(
            x_ref.at[rows], mod_ref, hist_ref, vecs_ref, w_in_ref, b_in_ref, taps_ref,
            w_a_ref, ws_ref, bs_ref, w_b_ref, w_o_ref, w_r_ref, b_r_ref,
            x1_ref.at[rows], h2_ref.at[rows], route_ref.at[rows], hist_out_ref,
            v_out_ref.at[rows] if emit_v else None, count_acc_ref, carry_ref,
            tile=sub, seg=min(seg, sub)))
    live = [True] * len(windows)
    step = 0
    while any(live):
        for k, window in enumerate(windows):
            if live[k] and step >= k * MIXER_SKEW:
                live[k] = next(window, None) is not None
        step += 1
    count_ref[...] = count_acc_ref[...]


def _mixer_stages(x_ref, mod_ref, hist_ref, vecs_ref, w_in_ref, b_in_ref, taps_ref, w_a_ref, ws_ref,
                  bs_ref, w_b_ref, w_o_ref, w_r_ref, b_r_ref, x1_ref, h2_ref, route_ref,
                  hist_out_ref, v_out_ref, count_acc_ref, carry_ref, *, tile, seg):
    has_hist = hist_ref is not None
    emit_v = v_out_ref is not None
    d = x_ref.shape[-1]
    n_seg = tile // seg
    chunk = min(seg, GMLP_CHUNK)
    n_chunk = tile // chunk
    conv_cols = d // CONV_COL_BLOCKS

    vecs = vecs_ref[...]
    conv_b, ln_g, ln_b, v_ln_g, v_ln_b, g_norm1, g_norm2 = [vecs[i:i + 1] for i in range(7)]
    mod = lambda k: _segment_rows(mod_ref, k * d, (k + 1) * d, n_seg, seg)
    proj = lambda lo, hi: (jnp.dot(hb, w_in_ref[:, lo * d:hi * d], preferred_element_type=F32)
                           + b_in_ref[:, lo * d:hi * d])

    x = x_ref[...]
    hb = (_rms_norm(x, g_norm1) * (1.0 + mod(1)) + mod(0)).astype(BF16)
    yield True

    za = proj(0, 2)
    yield True
    a = za[:, :d] * _sigmoid(za[:, d:])
    a_ext = []
    for g in range(n_seg):
        a_seg = a[g * seg:(g + 1) * seg]
        hist = hist_ref[g] if has_hist else carry_ref[...]
        a_ext.append(jnp.concatenate([hist, a_seg], axis=0))
        hist_out_ref[g] = a_seg[seg - HIST_PAD:]
    if not has_hist:
        carry_ref[...] = a[tile - HIST_PAD:]

    def conv_block(c):
        cols = slice(c * conv_cols, (c + 1) * conv_cols)
        parts = [_causal_conv(e, taps_ref, cols, seg) for e in a_ext]
        return parts[0] if n_seg == 1 else jnp.concatenate(parts, axis=0)
    yield True

    zb = proj(2, 6)
    zu, zv, zg_a, zg_b = [zb[:, k * d:(k + 1) * d] for k in range(4)]
    conv_parts = []
    for c in range(CONV_COL_BLOCKS):
        conv_parts.append(conv_block(c))
        yield True
    conv = conv_parts[0] if CONV_COL_BLOCKS == 1 else jnp.concatenate(conv_parts, axis=1)

    u = _gelu_tanh(zu)
    yield True
    v = _layer_norm(_gelu_tanh(zv), v_ln_g, v_ln_b)
    if emit_v:
        v_out_ref[...] = v
    vb = v.astype(BF16)
    yield True
    gd = d // GMLP_GROUPS
    causal = (lax.broadcasted_iota(jnp.int32, (chunk, chunk), 0)
              >= lax.broadcasted_iota(jnp.int32, (chunk, chunk), 1))
    mixed_cols = []
    for hg in range(GMLP_GROUPS):
        w = jnp.where(causal, ws_ref[hg], 0.0).astype(BF16)
        rhs = [vb[c * chunk:(c + 1) * chunk, hg * gd:(hg + 1) * gd] for c in range(n_chunk)]
        rhs = rhs[0] if n_chunk == 1 else jnp.concatenate(rhs, axis=1)
        out = jnp.dot(w, rhs, preferred_element_type=F32)
        cols = [out[:, c * gd:(c + 1) * gd] for c in range(n_chunk)]
        mixed_cols.append(cols[0] if n_chunk == 1 else jnp.concatenate(cols, axis=0))
    mixed = jnp.concatenate(mixed_cols, axis=1)
    ca = _layer_norm(conv + conv_b, ln_g, ln_b)
    ca = ca * _sigmoid(ca)
    yield True
    y_a = jnp.dot(ca.astype(BF16), w_a_ref[...], preferred_element_type=F32)
    bs = bs_ref[...]
    mixed = mixed + (bs if n_chunk == 1 else jnp.concatenate([bs] * n_chunk, axis=0))
    gated = (u * mixed).astype(BF16)
    yield True
    y_b = jnp.dot(gated, w_b_ref[...], preferred_element_type=F32)
    gate_a = _sigmoid(zg_a)
    gate_b = _sigmoid(zg_b)
    yield True

    merged = gate_a * y_a + gate_b * y_b
    m = jnp.dot(merged.astype(BF16), w_o_ref[...], preferred_element_type=F32)
    yield True
    x1 = x + mod(2) * m
    x1_ref[...] = x1

    h2 = _rms_norm(x1, g_norm2) * (1.0 + mod(4)) + mod(3)
    h2_ref[...] = h2.astype(h2_ref.dtype)
    logits = jnp.dot(h2.astype(BF16), w_r_ref[...], preferred_element_type=F32) + b_r_ref[...]
    yield True
    lane = lax.broadcasted_iota(jnp.int32, (tile, ROUTER_LANES), 1).astype(F32)
    is_group = lane < N_GROUPS
    lg = jnp.where(is_group, logits, NEG_BIG)
    mg = jnp.max(lg, axis=1, keepdims=True)
    gsel = jnp.min(jnp.where(lg == mg, lane, float(ROUTER_LANES)), axis=1, keepdims=True)
    pgsel = 1.0 / jnp.sum(jnp.where(is_group, jnp.exp(lg - mg), 0.0), axis=1, keepdims=True)
    lo = EXPERT_LANE0 + EXPERTS_PER_GROUP * gsel
    le = jnp.where((lane >= lo) & (lane < lo + EXPERTS_PER_GROUP), logits, NEG_BIG)
    m1 = jnp.max(le, axis=1, keepdims=True)
    i1 = jnp.min(jnp.where(le == m1, lane, float(ROUTER_LANES)), axis=1, keepdims=True)
    le2 = jnp.where(lane == i1, NEG_BIG, le)
    m2 = jnp.max(le2, axis=1, keepdims=True)
    i2 = jnp.min(jnp.where(le2 == m2, lane, float(ROUTER_LANES)), axis=1, keepdims=True)
    e2 = jnp.exp(m2 - m1)
    w1 = pgsel / (1.0 + e2)
    w2 = w1 * e2
    first_lower = i1 < i2
    ea = jnp.where(first_lower, i1, i2) - lo
    eb = jnp.where(first_lower, i2, i1) - lo
    bucket = gsel * PAIRS_PER_GROUP + ea * (2 * EXPERTS_PER_GROUP - 1 - ea) * 0.5 + (eb - ea - 1.0)
    wa = jnp.where(first_lower, w1, w2)
    wb = jnp.where(first_lower, w2, w1)
    route_ref[...] = (jnp.where(lane == i1, w1, 0.0) + jnp.where(lane == i2, w2, 0.0)
                      + jnp.where(lane == BUCKET_LANE, bucket, 0.0)
                      + jnp.where(lane == WA_LANE, wa, 0.0) + jnp.where(lane == WB_LANE, wb, 0.0))

    count_acc_ref[...] += jnp.sum(jnp.where(lane == bucket, 1.0, 0.0), axis=0, keepdims=True)


def _const_spec(shape):
    nd = len(shape)
    return pl.BlockSpec(shape, lambda i, t: (0,) * nd, pipeline_mode=pl.Buffered(1))


def _mixer(x, mod, hist, wts, *, tile, sub, seg, emit_v, h2_dtype):
    nb, s, d = x.shape
    n_seg = tile // seg
    has_hist = hist is not None
    grid = (nb, s // tile)
    chunk = min(seg, GMLP_CHUNK)

    in_specs = [pl.BlockSpec((None, tile, d), lambda i, t: (i, t, 0)),
                pl.BlockSpec((n_seg, 1, mod.shape[-1]), lambda i, t: (i, 0, 0))]
    args = [x, mod]
    if has_hist:
        in_specs.append(pl.BlockSpec((n_seg, HIST_PAD, d), lambda i, t: (i, 0, 0)))
        args.append(hist)
    consts = [wts["vecs"], wts["w_in"], wts["b_in"], wts["taps"], wts["w_a"],
              wts["w_s"][:, :chunk, :chunk], wts["b_s"][:chunk], wts["w_b"], wts["w_o"],
              wts["w_r"], wts["b_r"]]
    in_specs += [_const_spec(c.shape) for c in consts]
    args += consts

    out_shape = [jax.ShapeDtypeStruct((nb, s, d), F32),
                 jax.ShapeDtypeStruct((nb, s, d), h2_dtype),
                 jax.ShapeDtypeStruct((nb, s, ROUTER_LANES), F32),
                 jax.ShapeDtypeStruct((nb * n_seg, HIST_PAD, d), F32),
                 jax.ShapeDtypeStruct((SUBLANES, ROUTER_LANES), F32)]
    out_specs = [pl.BlockSpec((None, tile, d), lambda i, t: (i, t, 0)),
                 pl.BlockSpec((None, tile, d), lambda i, t: (i, t, 0)),
                 pl.BlockSpec((None, tile, ROUTER_LANES), lambda i, t: (i, t, 0)),
                 pl.BlockSpec((n_seg, HIST_PAD, d), lambda i, t: (i, 0, 0)),
                 pl.BlockSpec((SUBLANES, ROUTER_LANES), lambda i, t: (0, 0))]
    if emit_v:
        out_shape.append(jax.ShapeDtypeStruct((nb, s, d), F32))
        out_specs.append(pl.BlockSpec((None, tile, d), lambda i, t: (i, t, 0)))
    scratch = [pltpu.VMEM((SUBLANES, ROUTER_LANES), F32)]
    if not has_hist:
        scratch.append(pltpu.VMEM((HIST_PAD, d), F32))

    return pl.pallas_call(
        functools.partial(_mixer_kernel, tile=tile, sub=sub, seg=seg, has_hist=has_hist,
                          emit_v=emit_v),
        grid=grid, in_specs=in_specs, out_specs=out_specs, out_shape=out_shape,
        scratch_shapes=scratch,
        compiler_params=pltpu.CompilerParams(
            dimension_semantics=("arbitrary", "arbitrary"),
            vmem_limit_bytes=VMEM_LIMIT_BYTES),
    )(*args)


def _moe_kernel(h2_ref, comb_ref, x1_ref, mod_ref, modf_ref, gfin_ref, wg_ref, wu_ref, wd_ref,
                y_ref, acc_ref, *, tile, seg):
    e = pl.program_id(2)
    d = x1_ref.shape[-1]
    n_seg = tile // seg

    @pl.when(e == 0)
    def _():
        acc_ref[...] = jnp.zeros_like(acc_ref)

    hb = h2_ref[...]
    hg = jnp.dot(hb, wg_ref[...], preferred_element_type=F32)
    hu = jnp.dot(hb, wu_ref[...], preferred_element_type=F32)
    he = (hg * _sigmoid(hg)) * hu
    ye = jnp.dot(he.astype(BF16), wd_ref[...], preferred_element_type=F32)
    lane = lax.broadcasted_iota(jnp.int32, (tile, ROUTER_LANES), 1)
    cw = jnp.sum(jnp.where(lane == e + EXPERT_LANE0, comb_ref[...], 0.0), axis=1, keepdims=True)
    acc_ref[...] += cw * ye

    @pl.when(e == N_EXPERTS - 1)
    def _():
        g2 = _segment_rows(mod_ref, 5 * d, 6 * d, n_seg, seg)
        shf = _segment_rows(modf_ref, 0, d, n_seg, seg)
        scf = _segment_rows(modf_ref, d, 2 * d, n_seg, seg)
        x2 = x1_ref[...] + g2 * acc_ref[...]
        y_ref[...] = _rms_norm(x2, gfin_ref[...]) * (1.0 + scf) + shf


def _moe(h2, comb, x1, mod, modf, wts, *, tile, seg):
    nb, s, d = x1.shape
    n_seg = tile // seg
    de = wts["w_gate"].shape[-1]
    tok = lambda i, t, e: (i, t, 0)
    per_seq = lambda i, t, e: (i, 0, 0)
    return pl.pallas_call(
        functools.partial(_moe_kernel, tile=tile, seg=seg),
        grid=(nb, s // tile, N_EXPERTS),
        in_specs=[pl.BlockSpec((None, tile, d), tok),
                  pl.BlockSpec((None, tile, ROUTER_LANES), tok),
                  pl.BlockSpec((None, tile, d), tok),
                  pl.BlockSpec((n_seg, 1, mod.shape[-1]), per_seq),
                  pl.BlockSpec((n_seg, 1, modf.shape[-1]), per_seq),
                  pl.BlockSpec((1, d), lambda i, t, e: (0, 0)),
                  pl.BlockSpec((None, d, de), lambda i, t, e: (e, 0, 0)),
                  pl.BlockSpec((None, d, de), lambda i, t, e: (e, 0, 0)),
                  pl.BlockSpec((None, de, d), lambda i, t, e: (e, 0, 0))],
        out_specs=pl.BlockSpec((None, tile, d), tok),
        out_shape=jax.ShapeDtypeStruct((nb, s, d), F32),
        scratch_shapes=[pltpu.VMEM((tile, d), F32)],
        compiler_params=pltpu.CompilerParams(
            dimension_semantics=("arbitrary", "arbitrary", "arbitrary"),
            vmem_limit_bytes=VMEM_LIMIT_BYTES),
    )(h2, comb, x1, mod, modf, wts["g_final"], wts["w_gate"], wts["w_up"], wts["w_down"])


def _plan_kernel(route_ref, start_ref, slot_ref, next_ref, *, tile):
    @pl.when(pl.program_id(0) == 0)
    def _():
        next_ref[...] = start_ref[...]

    bucket_row = route_ref[...].T[BUCKET_LANE:BUCKET_LANE + 1, :]
    rows = lax.broadcasted_iota(jnp.int32, (ROUTER_LANES, tile), 0).astype(F32)
    member = rows == bucket_row
    upper = (lax.broadcasted_iota(jnp.int32, (tile, tile), 0)
             <= lax.broadcasted_iota(jnp.int32, (tile, tile), 1))
    seen = jnp.dot(jnp.where(member, 1.0, 0.0).astype(BF16), jnp.where(upper, 1.0, 0.0).astype(BF16),
                   preferred_element_type=F32)
    nxt = next_ref[...]
    slot = jnp.sum(jnp.where(member, seen - 1.0 + nxt[:, 0:1], 0.0), axis=0, keepdims=True)
    slot_ref[...] = slot.astype(jnp.int32)
    next_ref[...] = nxt + seen[:, tile - 1:tile]


def _plan(route, bucket_start, *, tile):
    n = route.shape[0]
    start = jnp.broadcast_to(bucket_start[:, None], (ROUTER_LANES, ROUTER_LANES))
    slots = pl.pallas_call(
        functools.partial(_plan_kernel, tile=tile),
        grid=(n // tile,),
        in_specs=[pl.BlockSpec((tile, ROUTER_LANES), lambda t: (t, 0)),
                  pl.BlockSpec((ROUTER_LANES, ROUTER_LANES), lambda t: (0, 0))],
        out_specs=pl.BlockSpec((None, 1, tile), lambda t: (t, 0, 0)),
        out_shape=jax.ShapeDtypeStruct((n // tile, 1, tile), jnp.int32),
        scratch_shapes=[pltpu.VMEM((ROUTER_LANES, ROUTER_LANES), F32)],
        compiler_params=pltpu.CompilerParams(dimension_semantics=("arbitrary",)),
    )(route, start)
    return slots.reshape(n)


def _dispatch_kernel(slot_ref, pad_tile_ref, src_ref, dst_hbm, zeros_ref, sem, *, tile, pad_rows):
    t = pl.program_id(0)

    @pl.when(t == 0)
    def _():
        zeros_ref[...] = jnp.zeros_like(zeros_ref)
        for phase in ("start", "wait"):
            for b in range(2 * N_BUCKETS):
                @pl.when(pad_tile_ref[b] >= 0)
                def _():
                    first_row = pl.multiple_of(pad_tile_ref[b], pad_rows)
                    copy = pltpu.make_async_copy(
                        zeros_ref, dst_hbm.at[pl.ds(first_row, pad_rows)], sem)
                    copy.start() if phase == "start" else copy.wait()

    base = t * tile
    for r in range(tile):
        pltpu.make_async_copy(src_ref.at[pl.ds(r, 1)], dst_hbm.at[pl.ds(slot_ref[base + r], 1)],
                              sem).start()
    pltpu.make_async_copy(src_ref, dst_hbm.at[pl.ds(0, tile)], sem).wait()


def _dispatch(slots, pad_tile, src, n_slots, *, tile, pad_rows):
    n, d = src.shape
    return pl.pallas_call(
        functools.partial(_dispatch_kernel, tile=tile, pad_rows=pad_rows),
        grid_spec=pltpu.PrefetchScalarGridSpec(
            num_scalar_prefetch=2, grid=(n // tile,),
            in_specs=[pl.BlockSpec((tile, d), lambda t, s, p: (t, 0))],
            out_specs=pl.BlockSpec(memory_space=pl.ANY),
            scratch_shapes=[pltpu.VMEM((pad_rows, d), src.dtype), pltpu.SemaphoreType.DMA(())]),
        out_shape=jax.ShapeDtypeStruct((n_slots, d), src.dtype),
        compiler_params=pltpu.CompilerParams(dimension_semantics=("arbitrary",),
                                             vmem_limit_bytes=VMEM_LIMIT_BYTES),
    )(slots, pad_tile, src)


def _grouped_moe_kernel(ea_ref, eb_ref, n_used_ref, hs_ref, wga_ref, wua_ref, wda_ref,
                        wgb_ref, wub_ref, wdb_ref, ys_ref):
    del ea_ref, eb_ref
    d = hs_ref.shape[-1]
    j = pl.program_id(0)

    @pl.when(j < n_used_ref[0])
    def _():
        xb = hs_ref[...].astype(BF16)
        for k, (wg, wu, wd) in enumerate(((wga_ref, wua_ref, wda_ref), (wgb_ref, wub_ref, wdb_ref))):
            g = jnp.dot(xb, wg[...], preferred_element_type=F32)
            u = jnp.dot(xb, wu[...], preferred_element_type=F32)
            he = (g * _sigmoid(g)) * u
            ys_ref[:, k * d:(k + 1) * d] = jnp.dot(he.astype(BF16), wd[...], preferred_element_type=F32)

    @pl.when(j >= n_used_ref[0])
    def _():
        ys_ref[...] = jnp.zeros_like(ys_ref)


def _grouped_moe(hs, tile_ea, tile_eb, n_used, wts, *, tile):
    n_slots, d = hs.shape
    de = wts["w_gate"].shape[-1]
    lower = lambda j, ea, eb, nu: (ea[j], 0, 0)
    higher = lambda j, ea, eb, nu: (eb[j], 0, 0)
    w_specs = lambda idx: [pl.BlockSpec((None, d, de), idx), pl.BlockSpec((None, d, de), idx),
                           pl.BlockSpec((None, de, d), idx)]
    w_args = [wts["w_gate"], wts["w_up"], wts["w_down"]]
    return pl.pallas_call(
        _grouped_moe_kernel,
        grid_spec=pltpu.PrefetchScalarGridSpec(
            num_scalar_prefetch=3, grid=(n_slots // tile,),
            in_specs=[pl.BlockSpec((tile, d), lambda j, ea, eb, nu: (jnp.minimum(j, nu[0] - 1), 0))]
                     + w_specs(lower) + w_specs(higher),
            out_specs=pl.BlockSpec((tile, 2 * d), lambda j, ea, eb, nu: (j, 0))),
        out_shape=jax.ShapeDtypeStruct((n_slots, 2 * d), F32),
        compiler_params=pltpu.CompilerParams(dimension_semantics=("arbitrary",),
                                             vmem_limit_bytes=VMEM_LIMIT_BYTES),
    )(tile_ea, tile_eb, n_used, hs, *w_args, *w_args)


def _combine_kernel(slot_ref, x1_ref, route_ref, mod_ref, modf_ref, gfin_ref, ys_hbm, y_ref,
                    ybuf, sems, *, tile, n_steps):
    t = pl.program_id(0)
    d = x1_ref.shape[-1]

    def fetch(step, buf):
        for r in range(tile):
            pltpu.make_async_copy(ys_hbm.at[pl.ds(slot_ref[step * tile + r], 1)],
                                  ybuf.at[buf, pl.ds(r, 1)], sems.at[buf]).start()

    def wait(buf):
        pltpu.make_async_copy(ys_hbm.at[pl.ds(0, tile)], ybuf.at[buf], sems.at[buf]).wait()

    @pl.when(t == 0)
    def _():
        fetch(0, 0)

    def step(buf):
        fetch((t + 1) % n_steps, 1 - buf)
        wait(buf)
        yy = ybuf[buf]
        route = route_ref[...]
        moe = route[:, WA_LANE:WA_LANE + 1] * yy[:, :d] + route[:, WB_LANE:WB_LANE + 1] * yy[:, d:]
        x2 = x1_ref[...] + mod_ref[0, :, 5 * d:6 * d] * moe
        y_ref[...] = (_rms_norm(x2, gfin_ref[...]) * (1.0 + modf_ref[0, :, d:2 * d])
                      + modf_ref[0, :, 0:d])

    for buf in range(2):
        pl.when(t % 2 == buf)(functools.partial(step, buf))

    @pl.when(t == n_steps - 1)
    def _():
        wait(n_steps % 2)


def _combine(slots, x1, route, mod, modf, ys, wts, *, tile, tiles_per_seq):
    n, d = x1.shape
    n_steps = n // tile
    tok = lambda t, s: (t, 0)
    per_seq = lambda t, s: (t // tiles_per_seq, 0, 0)
    return pl.pallas_call(
        functools.partial(_combine_kernel, tile=tile, n_steps=n_steps),
        grid_spec=pltpu.PrefetchScalarGridSpec(
            num_scalar_prefetch=1, grid=(n_steps,),
            in_specs=[pl.BlockSpec((tile, d), tok),
                      pl.BlockSpec((tile, ROUTER_LANES), tok),
                      pl.BlockSpec((1, 1, mod.shape[-1]), per_seq),
                      pl.BlockSpec((1, 1, modf.shape[-1]), per_seq),
                      pl.BlockSpec((1, d), lambda t, s: (0, 0)),
                      pl.BlockSpec(memory_space=pl.ANY)],
            out_specs=pl.BlockSpec((tile, d), tok),
            scratch_shapes=[pltpu.VMEM((2, tile, 2 * d), F32), pltpu.SemaphoreType.DMA((2,))]),
        out_shape=jax.ShapeDtypeStruct((n, d), F32),
        compiler_params=pltpu.CompilerParams(dimension_semantics=("arbitrary",),
                                             vmem_limit_bytes=VMEM_LIMIT_BYTES),
    )(slots, x1, route, mod, modf, wts["g_final"], ys)


def _sorted_moe(h2, x1, route, counts, mod, modf, wts, *, tiles_per_seq):
    n, d = x1.shape
    tile = SORT_TILE
    n_tiles = n // tile + N_BUCKETS
    cnt = counts[:N_BUCKETS].astype(jnp.int32)
    padded = (cnt + tile - 1) // tile * tile
    ends = jnp.cumsum(padded)
    starts = ends - padded
    n_used = ends[-1] // tile
    tile_start = jnp.arange(n_tiles, dtype=jnp.int32) * tile
    tile_bucket = jnp.sum((ends[None, :] <= tile_start[:, None]).astype(jnp.int32), axis=1)
    last_bucket = tile_bucket[n_used - 1]
    tile_bucket = jnp.where(jnp.arange(n_tiles) < n_used, tile_bucket, last_bucket)
    experts = jnp.asarray(_BUCKET_EXPERTS)[tile_bucket]
    bucket_start = jnp.zeros((ROUTER_LANES,), F32).at[:N_BUCKETS].set(starts.astype(F32))

    tail_tile = n_used + jnp.arange(N_BUCKETS, dtype=jnp.int32)
    pad_tile = jnp.concatenate([jnp.where(padded > cnt, ends - tile, -1),
                                jnp.where(tail_tile < n_tiles, tail_tile * tile, -1)])

    slots = _plan(route, bucket_start, tile=DISPATCH_TILE)
    hs = _dispatch(slots, pad_tile, h2, n_tiles * tile, tile=DISPATCH_TILE, pad_rows=tile)
    ys = _grouped_moe(hs, experts[:, 0], experts[:, 1], n_used.reshape(1), wts, tile=tile)
    return _combine(slots, x1, route, mod, modf, ys, wts, tile=tile, tiles_per_seq=tiles_per_seq)


def _layer_weights(l,w_in, b_in, conv_w, conv_b, ln_g, ln_b, w_a, v_ln_g, v_ln_b, w_s, b_s,
                   w_b, w_o, g_norm1, g_norm2, w_rg, b_rg, w_re, b_re, w_gate, w_up, w_down,
                   g_final):
    d = w_in.shape[1]
    gd = d // GMLP_GROUPS
    zero_row = jnp.zeros((1, d), F32)
    vecs = jnp.stack([conv_b[l], ln_g[l], ln_b[l], v_ln_g[l], v_ln_b[l], g_norm1[l], g_norm2[l],
                      zero_row[0]])
    taps = jnp.concatenate([conv_w[l][::-1], jnp.zeros((CONV_TAPS_PAD - CONV_WIDTH, d), F32)])
    taps = jnp.broadcast_to(taps[:, None, :], (CONV_TAPS_PAD, SUBLANES, d))
    pad = ROUTER_LANES - N_GROUPS - N_EXPERTS
    w_r = jnp.concatenate([w_rg[l], w_re[l], jnp.zeros((d, pad), F32)], axis=1)
    b_r = jnp.concatenate([b_rg[l], b_re[l], jnp.zeros((pad,), F32)]).reshape(1, ROUTER_LANES)
    return dict(
        vecs=vecs, taps=taps,
        w_in=w_in[l].astype(BF16), b_in=b_in[l].reshape(1, -1),
        w_a=w_a[l].astype(BF16), w_b=w_b[l].astype(BF16), w_o=w_o[l].astype(BF16),
        w_s=w_s[l], b_s=jnp.repeat(b_s[l].T, gd, axis=1),
        w_r=w_r.astype(BF16), b_r=b_r,
        w_gate=w_gate[l].astype(BF16), w_up=w_up[l].astype(BF16), w_down=w_down[l].astype(BF16),
        g_final=g_final.reshape(1, d))


def kernel(x_prompt, x_sample, cache_conv, c_prompt, c_sample, w_ada, b_ada, g_norm1, g_norm2, w_in, b_in, conv_w, conv_b, ln_g, ln_b, w_a, v_ln_g, v_ln_b, w_s, b_s, w_b, w_o, w_rg, b_rg, w_re, b_re, w_gate, w_up, w_down, g_final, w_ada_f, b_ada_f):
    depth = w_in.shape[0]
    assert depth == 1, "the streaming step is written for a single layer"
    l = 0
    batch, seq, d = x_prompt.shape
    dec_batch, dec_seq, _ = x_sample.shape
    assert dec_seq == HIST_PAD and seq % PROMPT_TILE == 0 and (dec_batch * dec_seq) % MIXER_TILE == 0
    assert (batch * seq) % DISPATCH_TILE == 0

    wts = _layer_weights(l, w_in, b_in, conv_w, conv_b, ln_g, ln_b, w_a, v_ln_g, v_ln_b, w_s, b_s,
                         w_b, w_o, g_norm1, g_norm2, w_rg, b_rg, w_re, b_re, w_gate, w_up, w_down,
                         g_final)

    c_all = jnp.concatenate([c_prompt, c_sample], axis=0)
    mod = _modulation(c_all, w_ada[l], b_ada[l])[:, None, :]
    modf = _modulation(c_all, w_ada_f, b_ada_f)[:, None, :]

    x1p, h2p, routep, histp, counts = _mixer(x_prompt, mod[:batch], None, wts, tile=PROMPT_TILE,
                                             sub=MIXER_TILE, seg=PROMPT_TILE, emit_v=False,
                                             h2_dtype=F32)
    n_prompt = batch * seq
    y_prompt = _sorted_moe(h2p.reshape(n_prompt, d), x1p.reshape(n_prompt, d),
                           routep.reshape(n_prompt, ROUTER_LANES), counts[0], mod[:batch],
                           modf[:batch], wts, tiles_per_seq=seq // SORT_TILE).reshape(batch, seq, d)

    n_tok = dec_batch * dec_seq
    xs = x_sample.reshape(n_tok // MIXER_TILE, MIXER_TILE, d)
    hist_s = jnp.pad(cache_conv[l], ((0, 0), (HIST_PAD - HIST, 0), (0, 0)))
    x1s, h2s, combs, hists, _, vs = _mixer(xs, mod[batch:], hist_s, wts, tile=MIXER_TILE,
                                           sub=MIXER_TILE, seg=dec_seq, emit_v=True, h2_dtype=BF16)
    moe_tile_s = min(MOE_TILE, n_tok)
    y_sample = _moe(h2s.reshape(n_tok // moe_tile_s, moe_tile_s, d),
                    combs.reshape(n_tok // moe_tile_s, moe_tile_s, ROUTER_LANES),
                    x1s.reshape(n_tok // moe_tile_s, moe_tile_s, d),
                    mod[batch:], modf[batch:], wts, tile=moe_tile_s, seg=dec_seq)

    state_conv_prompt = histp[None, :, HIST_PAD - HIST:, :]
    state_conv_sample = hists[None, :, HIST_PAD - HIST:, :]
    state_gmlp_v_sample = vs.reshape(1, dec_batch, dec_seq, d)
    return (y_prompt, y_sample.reshape(dec_batch, dec_seq, d), state_conv_prompt,
            state_conv_sample, state_gmlp_v_sample)
```

```python
import functools

import jax
import jax.numpy as jnp
import numpy as np
from jax import lax
from jax.experimental import pallas as pl
from jax.experimental.pallas import tpu as pltpu

F32 = jnp.float32
BF16 = jnp.bfloat16

EPS = 1e-6
CONV_WIDTH = 31
HIST = CONV_WIDTH - 1
SUBLANES = 8
LANES = 128
HIST_PAD = 32
CONV_TAPS_PAD = 32
GMLP_CHUNK = 128
GMLP_GROUPS = 8
N_GROUPS = 4
EXPERTS_PER_GROUP = 8
N_EXPERTS = N_GROUPS * EXPERTS_PER_GROUP
ROUTER_LANES = LANES
EXPERT_LANE0 = N_GROUPS
NEG_BIG = -1e30
VMEM_LIMIT_BYTES = 56 * 1024 * 1024

MIXER_TILE = 256
PROMPT_TILE = 512
CONV_COL_BLOCKS = 4
MIXER_SKEW = 1
MOE_TILE = 1024
SORT_TILE = 256
DISPATCH_TILE = 1024

PAIRS_PER_GROUP = EXPERTS_PER_GROUP * (EXPERTS_PER_GROUP - 1) // 2
N_BUCKETS = N_GROUPS * PAIRS_PER_GROUP
BUCKET_LANE = 64
WA_LANE = 65
WB_LANE = 66
_BUCKET_EXPERTS = np.array([(g * EXPERTS_PER_GROUP + a, g * EXPERTS_PER_GROUP + b)
                            for g in range(N_GROUPS)
                            for a in range(EXPERTS_PER_GROUP)
                            for b in range(a + 1, EXPERTS_PER_GROUP)], dtype=np.int32)


def _sigmoid(x):
    return 1.0 / (1.0 + jnp.exp2(x * -1.4426950408889634))


def _gelu_tanh(x):
    c = 0.7978845608028654
    inner = x * (c + (c * 0.044715) * (x * x))
    return (0.5 * x) * (1.0 + jnp.tanh(inner))


def _rms_norm(x, g):
    return (x * lax.rsqrt(jnp.mean(x * x, axis=-1, keepdims=True) + EPS)) * g


def _layer_norm(x, g, b):
    mu = jnp.mean(x, axis=-1, keepdims=True)
    xc = x - mu
    var = jnp.mean(xc * xc, axis=-1, keepdims=True)
    return xc * lax.rsqrt(var + EPS) * g + b


def _segment_rows(ref, lo, hi, n_seg, seg):
    if n_seg == 1:
        return ref[0, :, lo:hi]
    parts = [jnp.broadcast_to(ref[g, :, lo:hi], (seg, hi - lo)) for g in range(n_seg)]
    return jnp.concatenate(parts, axis=0)


def _modulation_kernel(c_ref, w_ref, b_ref, o_ref):
    c = c_ref[...]
    s = c * _sigmoid(c)
    o_ref[...] = jnp.dot(s, w_ref[...], precision=lax.Precision.HIGHEST,
                         preferred_element_type=F32) + b_ref[...]


def _modulation(c, w, b):
    n_seq, d = c.shape
    n_out = w.shape[1]
    bn = 1024
    return pl.pallas_call(
        _modulation_kernel,
        grid=(n_out // bn,),
        in_specs=[pl.BlockSpec((n_seq, d), lambda j: (0, 0)),
                  pl.BlockSpec((d, bn), lambda j: (0, j)),
                  pl.BlockSpec((1, bn), lambda j: (0, j))],
        out_specs=pl.BlockSpec((n_seq, bn), lambda j: (0, j)),
        out_shape=jax.ShapeDtypeStruct((n_seq, n_out), F32),
    )(c, w, b.reshape(1, n_out))


def _causal_conv(a_ext, taps_ref, cols, seg):
    n_q = CONV_TAPS_PAD // SUBLANES
    n_hist = HIST_PAD // SUBLANES
    group = lambda m: a_ext[SUBLANES * m:SUBLANES * (m + 1), cols]
    width = group(0).shape[-1]
    last_row = lax.broadcasted_iota(jnp.int32, (SUBLANES, width), 0) == SUBLANES - 1
    prev = [jnp.zeros((SUBLANES, width), F32)] * SUBLANES
    out = []
    for k in range(-1, seg // SUBLANES):
        operands = [group(n_hist + k - q) for q in range(n_q)]
        partial = [None] * SUBLANES
        for r in reversed(range(SUBLANES)):
            b_r = None
            for q in range(n_q):
                j = SUBLANES * q + r
                if j >= CONV_WIDTH:
                    continue
                term = taps_ref[j, :, cols] * operands[q]
                b_r = term if b_r is None else b_r + term
            if r < SUBLANES - 1:
                b_r = b_r + pltpu.roll(jnp.where(last_row, prev[r + 1], partial[r + 1]), 1, axis=0)
            partial[r] = b_r
        prev = partial
        if k >= 0:
            out.append(partial[0])
    return jnp.concatenate(out, axis=0)


def _mixer_kernel(*refs, tile, sub, seg, has_hist, emit_v):
    it = iter(refs)
    x_ref = next(it)
    mod_ref = next(it)
    hist_ref = next(it) if has_hist else None
    vecs_ref = next(it)
    w_in_ref = next(it)
    b_in_ref = next(it)
    taps_ref = next(it)
    w_a_ref = next(it)
    ws_ref = next(it)
    bs_ref = next(it)
    w_b_ref = next(it)
    w_o_ref = next(it)
    w_r_ref = next(it)
    b_r_ref = next(it)
    x1_ref = next(it)
    h2_ref = next(it)
    route_ref = next(it)
    hist_out_ref = next(it)
    count_ref = next(it)
    v_out_ref = next(it) if emit_v else None
    count_acc_ref = next(it)
    carry_ref = None if has_hist else next(it)

    if not has_hist:
        @pl.when(pl.program_id(1) == 0)
        def _():
            carry_ref[...] = jnp.zeros_like(carry_ref)

    @pl.when((pl.program_id(0) == 0) & (pl.program_id(1) == 0))
    def _():
        count_acc_ref[...] = jnp.zeros_like(count_acc_ref)

    windows = []
    for k in range(tile // sub):
        rows = pl.ds(k * sub, sub)
        windows.append(_mixer_stages(
            x_ref.at[rows], mod_ref, hist_ref, vecs_ref, w_in_ref, b_in_ref, taps_ref,
            w_a_ref, ws_ref, bs_ref, w_b_ref, w_o_ref, w_r_ref, b_r_ref,
            x1_ref.at[rows], h2_ref.at[rows], route_ref.at[rows], hist_out_ref,
            v_out_ref.at[rows] if emit_v else None, count_acc_ref, carry_ref,
            tile=sub, seg=min(seg, sub)))
    live = [True] * len(windows)
    step = 0
    while any(live):
        for k, window in enumerate(windows):
            if live[k] and step >= k * MIXER_SKEW:
                live[k] = next(window, None) is not None
        step += 1
    count_ref[...] = count_acc_ref[...]


def _mixer_stages(x_ref, mod_ref, hist_ref, vecs_ref, w_in_ref, b_in_ref, taps_ref, w_a_ref, ws_ref,
                  bs_ref, w_b_ref, w_o_ref, w_r_ref, b_r_ref, x1_ref, h2_ref, route_ref,
                  hist_out_ref, v_out_ref, count_acc_ref, carry_ref, *, tile, seg):
    has_hist = hist_ref is not None
    emit_v = v_out_ref is not None
    d = x_ref.shape[-1]
    n_seg = tile // seg
    chunk = min(seg, GMLP_CHUNK)
    n_chunk = tile // chunk
    conv_cols = d // CONV_COL_BLOCKS

    vecs = vecs_ref[...]
    conv_b, ln_g, ln_b, v_ln_g, v_ln_b, g_norm1, g_norm2 = [vecs[i:i + 1] for i in range(7)]
    mod = lambda k: _segment_rows(mod_ref, k * d, (k + 1) * d, n_seg, seg)
    proj = lambda lo, hi: (jnp.dot(hb, w_in_ref[:, lo * d:hi * d], preferred_element_type=F32)
                           + b_in_ref[:, lo * d:hi * d])

    x = x_ref[...]
    hb = (_rms_norm(x, g_norm1) * (1.0 + mod(1)) + mod(0)).astype(BF16)
    yield True

    za = proj(0, 2)
    yield True
    a = za[:, :d] * _sigmoid(za[:, d:])
    a_ext = []
    for g in range(n_seg):
        a_seg = a[g * seg:(g + 1) * seg]
        hist = hist_ref[g] if has_hist else carry_ref[...]
        a_ext.append(jnp.concatenate([hist, a_seg], axis=0))
        hist_out_ref[g] = a_seg[seg - HIST_PAD:]
    if not has_hist:
        carry_ref[...] = a[tile - HIST_PAD:]

    def conv_block(c):
        cols = slice(c * conv_cols, (c + 1) * conv_cols)
        parts = [_causal_conv(e, taps_ref, cols, seg) for e in a_ext]
        return parts[0] if n_seg == 1 else jnp.concatenate(parts, axis=0)
    yield True

    zb = proj(2, 6)
    zu, zv, zg_a, zg_b = [zb[:, k * d:(k + 1) * d] for k in range(4)]
    conv_parts = []
    for c in range(CONV_COL_BLOCKS):
        conv_parts.append(conv_block(c))
        yield True
    conv = conv_parts[0] if CONV_COL_BLOCKS == 1 else jnp.concatenate(conv_parts, axis=1)

    u = _gelu_tanh(zu)
    yield True
    v = _layer_norm(_gelu_tanh(zv), v_ln_g, v_ln_b)
    if emit_v:
        v_out_ref[...] = v
    vb = v.astype(BF16)
    yield True
    gd = d // GMLP_GROUPS
    causal = (lax.broadcasted_iota(jnp.int32, (chunk, chunk), 0)
              >= lax.broadcasted_iota(jnp.int32, (chunk, chunk), 1))
    mixed_cols = []
    for hg in range(GMLP_GROUPS):
        w = jnp.where(causal, ws_ref[hg], 0.0).astype(BF16)
        rhs = [vb[c * chunk:(c + 1) * chunk, hg * gd:(hg + 1) * gd] for c in range(n_chunk)]
        rhs = rhs[0] if n_chunk == 1 else jnp.concatenate(rhs, axis=1)
        out = jnp.dot(w, rhs, preferred_element_type=F32)
        cols = [out[:, c * gd:(c + 1) * gd] for c in range(n_chunk)]
        mixed_cols.append(cols[0] if n_chunk == 1 else jnp.concatenate(cols, axis=0))
    mixed = jnp.concatenate(mixed_cols, axis=1)
    ca = _layer_norm(conv + conv_b, ln_g, ln_b)
    ca = ca * _sigmoid(ca)
    yield True
    y_a = jnp.dot(ca.astype(BF16), w_a_ref[...], preferred_element_type=F32)
    bs = bs_ref[...]
    mixed = mixed + (bs if n_chunk == 1 else jnp.concatenate([bs] * n_chunk, axis=0))
    gated = (u * mixed).astype(BF16)
    yield True
    y_b = jnp.dot(gated, w_b_ref[...], preferred_element_type=F32)
    gate_a = _sigmoid(zg_a)
    gate_b = _sigmoid(zg_b)
    yield True

    merged = gate_a * y_a + gate_b * y_b
    m = jnp.dot(merged.astype(BF16), w_o_ref[...], preferred_element_type=F32)
    yield True
    x1 = x + mod(2) * m
    x1_ref[...] = x1

    h2 = _rms_norm(x1, g_norm2) * (1.0 + mod(4)) + mod(3)
    h2_ref[:, 0:d] = h2.astype(h2_ref.dtype)
    logits = jnp.dot(h2.astype(BF16), w_r_ref[...], preferred_element_type=F32) + b_r_ref[...]
    yield True
    lane = lax.broadcasted_iota(jnp.int32, (tile, ROUTER_LANES), 1).astype(F32)
    is_group = lane < N_GROUPS
    lg = jnp.where(is_group, logits, NEG_BIG)
    mg = jnp.max(lg, axis=1, keepdims=True)
    gsel = jnp.min(jnp.where(lg == mg, lane, float(ROUTER_LANES)), axis=1, keepdims=True)
    pgsel = 1.0 / jnp.sum(jnp.where(is_group, jnp.exp(lg - mg), 0.0), axis=1, keepdims=True)
    lo = EXPERT_LANE0 + EXPERTS_PER_GROUP * gsel
    le = jnp.where((lane >= lo) & (lane < lo + EXPERTS_PER_GROUP), logits, NEG_BIG)
    m1 = jnp.max(le, axis=1, keepdims=True)
    i1 = jnp.min(jnp.where(le == m1, lane, float(ROUTER_LANES)), axis=1, keepdims=True)
    le2 = jnp.where(lane == i1, NEG_BIG, le)
    m2 = jnp.max(le2, axis=1, keepdims=True)
    i2 = jnp.min(jnp.where(le2 == m2, lane, float(ROUTER_LANES)), axis=1, keepdims=True)
    e2 = jnp.exp(m2 - m1)
    w1 = pgsel / (1.0 + e2)
    w2 = w1 * e2
    first_lower = i1 < i2
    ea = jnp.where(first_lower, i1, i2) - lo
    eb = jnp.where(first_lower, i2, i1) - lo
    bucket = gsel * PAIRS_PER_GROUP + ea * (2 * EXPERTS_PER_GROUP - 1 - ea) * 0.5 + (eb - ea - 1.0)
    wa = jnp.where(first_lower, w1, w2)
    wb = jnp.where(first_lower, w2, w1)
    route = (jnp.where(lane == i1, w1, 0.0) + jnp.where(lane == i2, w2, 0.0)
             + jnp.where(lane == BUCKET_LANE, bucket, 0.0)
             + jnp.where(lane == WA_LANE, wa, 0.0) + jnp.where(lane == WB_LANE, wb, 0.0))
    route_ref[...] = route
    if h2_ref.shape[-1] > d:
        h2_ref[:, d:d + ROUTER_LANES] = route

    count_acc_ref[...] += jnp.sum(jnp.where(lane == bucket, 1.0, 0.0), axis=0, keepdims=True)


def _const_spec(shape):
    nd = len(shape)
    return pl.BlockSpec(shape, lambda i, t: (0,) * nd, pipeline_mode=pl.Buffered(1))


def _mixer(x, mod, hist, wts, *, tile, sub, seg, emit_v, h2_dtype, h2_with_route):
    nb, s, d = x.shape
    n_seg = tile // seg
    has_hist = hist is not None
    grid = (nb, s // tile)
    chunk = min(seg, GMLP_CHUNK)

    in_specs = [pl.BlockSpec((None, tile, d), lambda i, t: (i, t, 0)),
                pl.BlockSpec((n_seg, 1, mod.shape[-1]), lambda i, t: (i, 0, 0))]
    args = [x, mod]
    if has_hist:
        in_specs.append(pl.BlockSpec((n_seg, HIST_PAD, d), lambda i, t: (i, 0, 0)))
        args.append(hist)
    consts = [wts["vecs"], wts["w_in"], wts["b_in"], wts["taps"], wts["w_a"],
              wts["w_s"][:, :chunk, :chunk], wts["b_s"][:chunk], wts["w_b"], wts["w_o"],
              wts["w_r"], wts["b_r"]]
    in_specs += [_const_spec(c.shape) for c in consts]
    args += consts

    h2_width = d + ROUTER_LANES if h2_with_route else d
    out_shape = [jax.ShapeDtypeStruct((nb, s, d), F32),
                 jax.ShapeDtypeStruct((nb, s, h2_width), h2_dtype),
                 jax.ShapeDtypeStruct((nb, s, ROUTER_LANES), F32),
                 jax.ShapeDtypeStruct((nb * n_seg, HIST_PAD, d), F32),
                 jax.ShapeDtypeStruct((SUBLANES, ROUTER_LANES), F32)]
    out_specs = [pl.BlockSpec((None, tile, d), lambda i, t: (i, t, 0)),
                 pl.BlockSpec((None, tile, h2_width), lambda i, t: (i, t, 0)),
                 pl.BlockSpec((None, tile, ROUTER_LANES), lambda i, t: (i, t, 0)),
                 pl.BlockSpec((n_seg, HIST_PAD, d), lambda i, t: (i, 0, 0)),
                 pl.BlockSpec((SUBLANES, ROUTER_LANES), lambda i, t: (0, 0))]
    if emit_v:
        out_shape.append(jax.ShapeDtypeStruct((nb, s, d), F32))
        out_specs.append(pl.BlockSpec((None, tile, d), lambda i, t: (i, t, 0)))
    scratch = [pltpu.VMEM((SUBLANES, ROUTER_LANES), F32)]
    if not has_hist:
        scratch.append(pltpu.VMEM((HIST_PAD, d), F32))

    return pl.pallas_call(
        functools.partial(_mixer_kernel, tile=tile, sub=sub, seg=seg, has_hist=has_hist,
                          emit_v=emit_v),
        grid=grid, in_specs=in_specs, out_specs=out_specs, out_shape=out_shape,
        scratch_shapes=scratch,
        compiler_params=pltpu.CompilerParams(
            dimension_semantics=("arbitrary", "arbitrary"),
            vmem_limit_bytes=VMEM_LIMIT_BYTES),
    )(*args)


def _moe_kernel(h2_ref, comb_ref, x1_ref, mod_ref, modf_ref, gfin_ref, wg_ref, wu_ref, wd_ref,
                y_ref, acc_ref, *, tile, seg):
    e = pl.program_id(2)
    d = x1_ref.shape[-1]
    n_seg = tile // seg

    @pl.when(e == 0)
    def _():
        acc_ref[...] = jnp.zeros_like(acc_ref)

    hb = h2_ref[...]
    hg = jnp.dot(hb, wg_ref[...], preferred_element_type=F32)
    hu = jnp.dot(hb, wu_ref[...], preferred_element_type=F32)
    he = (hg * _sigmoid(hg)) * hu
    ye = jnp.dot(he.astype(BF16), wd_ref[...], preferred_element_type=F32)
    lane = lax.broadcasted_iota(jnp.int32, (tile, ROUTER_LANES), 1)
    cw = jnp.sum(jnp.where(lane == e + EXPERT_LANE0, comb_ref[...], 0.0), axis=1, keepdims=True)
    acc_ref[...] += cw * ye

    @pl.when(e == N_EXPERTS - 1)
    def _():
        g2 = _segment_rows(mod_ref, 5 * d, 6 * d, n_seg, seg)
        shf = _segment_rows(modf_ref, 0, d, n_seg, seg)
        scf = _segment_rows(modf_ref, d, 2 * d, n_seg, seg)
        x2 = x1_ref[...] + g2 * acc_ref[...]
        y_ref[...] = _rms_norm(x2, gfin_ref[...]) * (1.0 + scf) + shf


def _moe(h2, comb, x1, mod, modf, wts, *, tile, seg):
    nb, s, d = x1.shape
    n_seg = tile // seg
    de = wts["w_gate"].shape[-1]
    tok = lambda i, t, e: (i, t, 0)
    per_seq = lambda i, t, e: (i, 0, 0)
    return pl.pallas_call(
        functools.partial(_moe_kernel, tile=tile, seg=seg),
        grid=(nb, s // tile, N_EXPERTS),
        in_specs=[pl.BlockSpec((None, tile, d), tok),
                  pl.BlockSpec((None, tile, ROUTER_LANES), tok),
                  pl.BlockSpec((None, tile, d), tok),
                  pl.BlockSpec((n_seg, 1, mod.shape[-1]), per_seq),
                  pl.BlockSpec((n_seg, 1, modf.shape[-1]), per_seq),
                  pl.BlockSpec((1, d), lambda i, t, e: (0, 0)),
                  pl.BlockSpec((None, d, de), lambda i, t, e: (e, 0, 0)),
                  pl.BlockSpec((None, d, de), lambda i, t, e: (e, 0, 0)),
                  pl.BlockSpec((None, de, d), lambda i, t, e: (e, 0, 0))],
        out_specs=pl.BlockSpec((None, tile, d), tok),
        out_shape=jax.ShapeDtypeStruct((nb, s, d), F32),
        scratch_shapes=[pltpu.VMEM((tile, d), F32)],
        compiler_params=pltpu.CompilerParams(
            dimension_semantics=("arbitrary", "arbitrary", "arbitrary"),
            vmem_limit_bytes=VMEM_LIMIT_BYTES),
    )(h2, comb, x1, mod, modf, wts["g_final"], wts["w_gate"], wts["w_up"], wts["w_down"])


def _plan_kernel(route_ref, start_ref, slot_ref, next_ref, *, tile):
    @pl.when(pl.program_id(0) == 0)
    def _():
        next_ref[...] = start_ref[...]

    bucket_row = route_ref[...].T[BUCKET_LANE:BUCKET_LANE + 1, :]
    rows = lax.broadcasted_iota(jnp.int32, (ROUTER_LANES, tile), 0).astype(F32)
    member = rows == bucket_row
    upper = (lax.broadcasted_iota(jnp.int32, (tile, tile), 0)
             <= lax.broadcasted_iota(jnp.int32, (tile, tile), 1))
    seen = jnp.dot(jnp.where(member, 1.0, 0.0).astype(BF16), jnp.where(upper, 1.0, 0.0).astype(BF16),
                   preferred_element_type=F32)
    nxt = next_ref[...]
    slot = jnp.sum(jnp.where(member, seen - 1.0 + nxt[:, 0:1], 0.0), axis=0, keepdims=True)
    slot_ref[...] = slot.astype(jnp.int32)
    next_ref[...] = nxt + seen[:, tile - 1:tile]


def _plan(route, bucket_start, *, tile):
    n = route.shape[0]
    start = jnp.broadcast_to(bucket_start[:, None], (ROUTER_LANES, ROUTER_LANES))
    slots = pl.pallas_call(
        functools.partial(_plan_kernel, tile=tile),
        grid=(n // tile,),
        in_specs=[pl.BlockSpec((tile, ROUTER_LANES), lambda t: (t, 0)),
                  pl.BlockSpec((ROUTER_LANES, ROUTER_LANES), lambda t: (0, 0))],
        out_specs=pl.BlockSpec((None, 1, tile), lambda t: (t, 0, 0)),
        out_shape=jax.ShapeDtypeStruct((n // tile, 1, tile), jnp.int32),
        scratch_shapes=[pltpu.VMEM((ROUTER_LANES, ROUTER_LANES), F32)],
        compiler_params=pltpu.CompilerParams(dimension_semantics=("arbitrary",)),
    )(route, start)
    return slots.reshape(n)


def _dispatch_kernel(slot_ref, pad_tile_ref, src_ref, dst_hbm, zeros_ref, sem, *, tile, pad_rows):
    t = pl.program_id(0)

    @pl.when(t == 0)
    def _():
        zeros_ref[...] = jnp.zeros_like(zeros_ref)
        for phase in ("start", "wait"):
            for b in range(2 * N_BUCKETS):
                @pl.when(pad_tile_ref[b] >= 0)
                def _():
                    first_row = pl.multiple_of(pad_tile_ref[b], pad_rows)
                    copy = pltpu.make_async_copy(
                        zeros_ref, dst_hbm.at[pl.ds(first_row, pad_rows)], sem)
                    copy.start() if phase == "start" else copy.wait()

    base = t * tile
    for r in range(tile):
        pltpu.make_async_copy(src_ref.at[pl.ds(r, 1)], dst_hbm.at[pl.ds(slot_ref[base + r], 1)],
                              sem).start()
    pltpu.make_async_copy(src_ref, dst_hbm.at[pl.ds(0, tile)], sem).wait()


def _dispatch(slots, pad_tile, src, n_slots, *, tile, pad_rows):
    n, d = src.shape
    return pl.pallas_call(
        functools.partial(_dispatch_kernel, tile=tile, pad_rows=pad_rows),
        grid_spec=pltpu.PrefetchScalarGridSpec(
            num_scalar_prefetch=2, grid=(n // tile,),
            in_specs=[pl.BlockSpec((tile, d), lambda t, s, p: (t, 0))],
            out_specs=pl.BlockSpec(memory_space=pl.ANY),
            scratch_shapes=[pltpu.VMEM((pad_rows, d), src.dtype), pltpu.SemaphoreType.DMA(())]),
        out_shape=jax.ShapeDtypeStruct((n_slots, d), src.dtype),
        compiler_params=pltpu.CompilerParams(dimension_semantics=("arbitrary",),
                                             vmem_limit_bytes=VMEM_LIMIT_BYTES),
    )(slots, pad_tile, src)


def _grouped_moe_kernel(ea_ref, eb_ref, n_used_ref, hs_ref, wga_ref, wua_ref, wda_ref,
                        wgb_ref, wub_ref, wdb_ref, ys_ref):
    del ea_ref, eb_ref
    d = ys_ref.shape[-1]
    j = pl.program_id(0)

    @pl.when(j < n_used_ref[0])
    def _():
        xb = hs_ref[:, 0:d].astype(BF16)
        record = hs_ref[:, d:d + ROUTER_LANES]
        acc = None
        for lane, (wg, wu, wd) in ((WA_LANE, (wga_ref, wua_ref, wda_ref)),
                                   (WB_LANE, (wgb_ref, wub_ref, wdb_ref))):
            g = jnp.dot(xb, wg[...], preferred_element_type=F32)
            u = jnp.dot(xb, wu[...], preferred_element_type=F32)
            he = (g * _sigmoid(g)) * u
            ye = jnp.dot(he.astype(BF16), wd[...], preferred_element_type=F32)
            term = record[:, lane:lane + 1] * ye
            acc = term if acc is None else acc + term
        ys_ref[...] = acc

    @pl.when(j >= n_used_ref[0])
    def _():
        ys_ref[...] = jnp.zeros_like(ys_ref)


def _grouped_moe(hs, tile_ea, tile_eb, n_used, wts, *, tile):
    n_slots, width = hs.shape
    d = width - ROUTER_LANES
    de = wts["w_gate"].shape[-1]
    lower = lambda j, ea, eb, nu: (ea[j], 0, 0)
    higher = lambda j, ea, eb, nu: (eb[j], 0, 0)
    w_specs = lambda idx: [pl.BlockSpec((None, d, de), idx), pl.BlockSpec((None, d, de), idx),
                           pl.BlockSpec((None, de, d), idx)]
    w_args = [wts["w_gate"], wts["w_up"], wts["w_down"]]
    return pl.pallas_call(
        _grouped_moe_kernel,
        grid_spec=pltpu.PrefetchScalarGridSpec(
            num_scalar_prefetch=3, grid=(n_slots // tile,),
            in_specs=[pl.BlockSpec((tile, width),
                                   lambda j, ea, eb, nu: (jnp.minimum(j, nu[0] - 1), 0))]
                     + w_specs(lower) + w_specs(higher),
            out_specs=pl.BlockSpec((tile, d), lambda j, ea, eb, nu: (j, 0))),
        out_shape=jax.ShapeDtypeStruct((n_slots, d), F32),
        compiler_params=pltpu.CompilerParams(dimension_semantics=("arbitrary",),
                                             vmem_limit_bytes=VMEM_LIMIT_BYTES),
    )(tile_ea, tile_eb, n_used, hs, *w_args, *w_args)


def _combine_kernel(slot_ref, x1_ref, mod_ref, modf_ref, gfin_ref, ys_hbm, y_ref,
                    ybuf, sems, *, tile, n_steps):
    t = pl.program_id(0)
    d = x1_ref.shape[-1]

    def fetch(step, buf):
        for r in range(tile):
            pltpu.make_async_copy(ys_hbm.at[pl.ds(slot_ref[step * tile + r], 1)],
                                  ybuf.at[buf, pl.ds(r, 1)], sems.at[buf]).start()

    def wait(buf):
        pltpu.make_async_copy(ys_hbm.at[pl.ds(0, tile)], ybuf.at[buf], sems.at[buf]).wait()

    @pl.when(t == 0)
    def _():
        fetch(0, 0)

    def step(buf):
        fetch((t + 1) % n_steps, 1 - buf)
        wait(buf)
        x2 = x1_ref[...] + mod_ref[0, :, 5 * d:6 * d] * ybuf[buf]
        y_ref[...] = (_rms_norm(x2, gfin_ref[...]) * (1.0 + modf_ref[0, :, d:2 * d])
                      + modf_ref[0, :, 0:d])

    for buf in range(2):
        pl.when(t % 2 == buf)(functools.partial(step, buf))

    @pl.when(t == n_steps - 1)
    def _():
        wait(n_steps % 2)


def _combine(slots, x1, mod, modf, ys, wts, *, tile, tiles_per_seq):
    n, d = x1.shape
    n_steps = n // tile
    tok = lambda t, s: (t, 0)
    per_seq = lambda t, s: (t // tiles_per_seq, 0, 0)
    return pl.pallas_call(
        functools.partial(_combine_kernel, tile=tile, n_steps=n_steps),
        grid_spec=pltpu.PrefetchScalarGridSpec(
            num_scalar_prefetch=1, grid=(n_steps,),
            in_specs=[pl.BlockSpec((tile, d), tok),
                      pl.BlockSpec((1, 1, mod.shape[-1]), per_seq),
                      pl.BlockSpec((1, 1, modf.shape[-1]), per_seq),
                      pl.BlockSpec((1, d), lambda t, s: (0, 0)),
                      pl.BlockSpec(memory_space=pl.ANY)],
            out_specs=pl.BlockSpec((tile, d), tok),
            scratch_shapes=[pltpu.VMEM((2, tile, d), F32), pltpu.SemaphoreType.DMA((2,))]),
        out_shape=jax.ShapeDtypeStruct((n, d), F32),
        compiler_params=pltpu.CompilerParams(dimension_semantics=("arbitrary",),
                                             vmem_limit_bytes=VMEM_LIMIT_BYTES),
    )(slots, x1, mod, modf, wts["g_final"], ys)


def _sorted_moe(h2, x1, route, counts, mod, modf, wts, *, tiles_per_seq):
    n, d = x1.shape
    tile = SORT_TILE
    n_tiles = n // tile + N_BUCKETS
    cnt = counts[:N_BUCKETS].astype(jnp.int32)
    padded = (cnt + tile - 1) // tile * tile
    ends = jnp.cumsum(padded)
    starts = ends - padded
    n_used = ends[-1] // tile
    tile_start = jnp.arange(n_tiles, dtype=jnp.int32) * tile
    tile_bucket = jnp.sum((ends[None, :] <= tile_start[:, None]).astype(jnp.int32), axis=1)
    last_bucket = tile_bucket[n_used - 1]
    tile_bucket = jnp.where(jnp.arange(n_tiles) < n_used, tile_bucket, last_bucket)
    experts = jnp.asarray(_BUCKET_EXPERTS)[tile_bucket]
    bucket_start = jnp.zeros((ROUTER_LANES,), F32).at[:N_BUCKETS].set(starts.astype(F32))

    tail_tile = n_used + jnp.arange(N_BUCKETS, dtype=jnp.int32)
    pad_tile = jnp.concatenate([jnp.where(padded > cnt, ends - tile, -1),
                                jnp.where(tail_tile < n_tiles, tail_tile * tile, -1)])

    slots = _plan(route, bucket_start, tile=DISPATCH_TILE)
    hs = _dispatch(slots, pad_tile, h2, n_tiles * tile, tile=DISPATCH_TILE, pad_rows=tile)
    ys = _grouped_moe(hs, experts[:, 0], experts[:, 1], n_used.reshape(1), wts, tile=tile)
    return _combine(slots, x1, mod, modf, ys, wts, tile=tile, tiles_per_seq=tiles_per_seq)


def _layer_weights(l,w_in, b_in, conv_w, conv_b, ln_g, ln_b, w_a, v_ln_g, v_ln_b, w_s, b_s,
                   w_b, w_o, g_norm1, g_norm2, w_rg, b_rg, w_re, b_re, w_gate, w_up, w_down,
                   g_final):
    d = w_in.shape[1]
    gd = d // GMLP_GROUPS
    zero_row = jnp.zeros((1, d), F32)
    vecs = jnp.stack([conv_b[l], ln_g[l], ln_b[l], v_ln_g[l], v_ln_b[l], g_norm1[l], g_norm2[l],
                      zero_row[0]])
    taps = jnp.concatenate([conv_w[l][::-1], jnp.zeros((CONV_TAPS_PAD - CONV_WIDTH, d), F32)])
    taps = jnp.broadcast_to(taps[:, None, :], (CONV_TAPS_PAD, SUBLANES, d))
    pad = ROUTER_LANES - N_GROUPS - N_EXPERTS
    w_r = jnp.concatenate([w_rg[l], w_re[l], jnp.zeros((d, pad), F32)], axis=1)
    b_r = jnp.concatenate([b_rg[l], b_re[l], jnp.zeros((pad,), F32)]).reshape(1, ROUTER_LANES)
    return dict(
        vecs=vecs, taps=taps,
        w_in=w_in[l].astype(BF16), b_in=b_in[l].reshape(1, -1),
        w_a=w_a[l].astype(BF16), w_b=w_b[l].astype(BF16), w_o=w_o[l].astype(BF16),
        w_s=w_s[l], b_s=jnp.repeat(b_s[l].T, gd, axis=1),
        w_r=w_r.astype(BF16), b_r=b_r,
        w_gate=w_gate[l].astype(BF16), w_up=w_up[l].astype(BF16), w_down=w_down[l].astype(BF16),
        g_final=g_final.reshape(1, d))


def kernel(x_prompt, x_sample, cache_conv, c_prompt, c_sample, w_ada, b_ada, g_norm1, g_norm2, w_in, b_in, conv_w, conv_b, ln_g, ln_b, w_a, v_ln_g, v_ln_b, w_s, b_s, w_b, w_o, w_rg, b_rg, w_re, b_re, w_gate, w_up, w_down, g_final, w_ada_f, b_ada_f):
    depth = w_in.shape[0]
    assert depth == 1, "the streaming step is written for a single layer"
    l = 0
    batch, seq, d = x_prompt.shape
    dec_batch, dec_seq, _ = x_sample.shape
    assert dec_seq == HIST_PAD and seq % PROMPT_TILE == 0 and (dec_batch * dec_seq) % MIXER_TILE == 0
    assert (batch * seq) % DISPATCH_TILE == 0

    wts = _layer_weights(l, w_in, b_in, conv_w, conv_b, ln_g, ln_b, w_a, v_ln_g, v_ln_b, w_s, b_s,
                         w_b, w_o, g_norm1, g_norm2, w_rg, b_rg, w_re, b_re, w_gate, w_up, w_down,
                         g_final)

    c_all = jnp.concatenate([c_prompt, c_sample], axis=0)
    mod = _modulation(c_all, w_ada[l], b_ada[l])[:, None, :]
    modf = _modulation(c_all, w_ada_f, b_ada_f)[:, None, :]

    x1p, h2p, routep, histp, counts = _mixer(x_prompt, mod[:batch], None, wts, tile=PROMPT_TILE,
                                             sub=MIXER_TILE, seg=PROMPT_TILE, emit_v=False,
                                             h2_dtype=F32, h2_with_route=True)
    n_prompt = batch * seq
    y_prompt = _sorted_moe(h2p.reshape(n_prompt, d + ROUTER_LANES), x1p.reshape(n_prompt, d),
                           routep.reshape(n_prompt, ROUTER_LANES), counts[0], mod[:batch],
                           modf[:batch], wts, tiles_per_seq=seq // SORT_TILE).reshape(batch, seq, d)

    n_tok = dec_batch * dec_seq
    xs = x_sample.reshape(n_tok // MIXER_TILE, MIXER_TILE, d)
    hist_s = jnp.pad(cache_conv[l], ((0, 0), (HIST_PAD - HIST, 0), (0, 0)))
    x1s, h2s, combs, hists, _, vs = _mixer(xs, mod[batch:], hist_s, wts, tile=MIXER_TILE,
                                           sub=MIXER_TILE, seg=dec_seq, emit_v=True, h2_dtype=BF16,
                                           h2_with_route=False)
    moe_tile_s = min(MOE_TILE, n_tok)
    y_sample = _moe(h2s.reshape(n_tok // moe_tile_s, moe_tile_s, d),
                    combs.reshape(n_tok // moe_tile_s, moe_tile_s, ROUTER_LANES),
                    x1s.reshape(n_tok // moe_tile_s, moe_tile_s, d),
                    mod[batch:], modf[batch:], wts, tile=moe_tile_s, seg=dec_seq)

    state_conv_prompt = histp[None, :, HIST_PAD - HIST:, :]
    state_conv_sample = hists[None, :, HIST_PAD - HIST:, :]
    state_gmlp_v_sample = vs.reshape(1, dec_batch, dec_seq, d)
    return (y_prompt, y_sample.reshape(dec_batch, dec_seq, d), state_conv_prompt,
            state_conv_sample, state_gmlp_v_sample)
```

```python
import functools

import jax
import jax.numpy as jnp
import numpy as np
from jax import lax
from jax.experimental import pallas as pl
from jax.experimental.pallas import tpu as pltpu

F32 = jnp.float32
BF16 = jnp.bfloat16

EPS = 1e-6
CONV_WIDTH = 31
HIST = CONV_WIDTH - 1
SUBLANES = 8
LANES = 128
HIST_PAD = 32
CONV_TAPS_PAD = 32
GMLP_CHUNK = 128
GMLP_GROUPS = 8
N_GROUPS = 4
EXPERTS_PER_GROUP = 8
N_EXPERTS = N_GROUPS * EXPERTS_PER_GROUP
ROUTER_LANES = LANES
EXPERT_LANE0 = N_GROUPS
NEG_BIG = -1e30
VMEM_LIMIT_BYTES = 56 * 1024 * 1024

MIXER_TILE = 256
PROMPT_TILE = 512
CONV_COL_BLOCKS = 4
MIXER_SKEW = 1
MOE_TILE = 1024
SORT_TILE = 256
PLAN_TILE = 1024
DISPATCH_TILE = 2048
COMBINE_TILE = 512

PAIRS_PER_GROUP = EXPERTS_PER_GROUP * (EXPERTS_PER_GROUP - 1) // 2
N_BUCKETS = N_GROUPS * PAIRS_PER_GROUP
BUCKET_LANE = 64
WA_LANE = 65
WB_LANE = 66
_BUCKET_EXPERTS = np.array([(g * EXPERTS_PER_GROUP + a, g * EXPERTS_PER_GROUP + b)
                            for g in range(N_GROUPS)
                            for a in range(EXPERTS_PER_GROUP)
                            for b in range(a + 1, EXPERTS_PER_GROUP)], dtype=np.int32)


def _sigmoid(x):
    return 1.0 / (1.0 + jnp.exp2(x * -1.4426950408889634))


def _gelu_tanh(x):
    c = 0.7978845608028654
    inner = x * (c + (c * 0.044715) * (x * x))
    return (0.5 * x) * (1.0 + jnp.tanh(inner))


def _rms_norm(x, g):
    return (x * lax.rsqrt(jnp.mean(x * x, axis=-1, keepdims=True) + EPS)) * g


def _layer_norm(x, g, b):
    mu = jnp.mean(x, axis=-1, keepdims=True)
    xc = x - mu
    var = jnp.mean(xc * xc, axis=-1, keepdims=True)
    return xc * lax.rsqrt(var + EPS) * g + b


def _segment_rows(ref, lo, hi, n_seg, seg):
    if n_seg == 1:
        return ref[0, :, lo:hi]
    parts = [jnp.broadcast_to(ref[g, :, lo:hi], (seg, hi - lo)) for g in range(n_seg)]
    return jnp.concatenate(parts, axis=0)


def _modulation_kernel(c_ref, w_ref, b_ref, o_ref):
    c = c_ref[...]
    s = c * _sigmoid(c)
    o_ref[...] = jnp.dot(s, w_ref[...], precision=lax.Precision.HIGHEST,
                         preferred_element_type=F32) + b_ref[...]


def _modulation(c, w, b):
    n_seq, d = c.shape
    n_out = w.shape[1]
    bn = 1024
    return pl.pallas_call(
        _modulation_kernel,
        grid=(n_out // bn,),
        in_specs=[pl.BlockSpec((n_seq, d), lambda j: (0, 0)),
                  pl.BlockSpec((d, bn), lambda j: (0, j)),
                  pl.BlockSpec((1, bn), lambda j: (0, j))],
        out_specs=pl.BlockSpec((n_seq, bn), lambda j: (0, j)),
        out_shape=jax.ShapeDtypeStruct((n_seq, n_out), F32),
    )(c, w, b.reshape(1, n_out))


def _causal_conv(a_ext, taps_ref, cols, seg):
    n_q = CONV_TAPS_PAD // SUBLANES
    n_hist = HIST_PAD // SUBLANES
    group = lambda m: a_ext[SUBLANES * m:SUBLANES * (m + 1), cols]
    width = group(0).shape[-1]
    last_row = lax.broadcasted_iota(jnp.int32, (SUBLANES, width), 0) == SUBLANES - 1
    prev = [jnp.zeros((SUBLANES, width), F32)] * SUBLANES
    out = []
    for k in range(-1, seg // SUBLANES):
        operands = [group(n_hist + k - q) for q in range(n_q)]
        partial = [None] * SUBLANES
        for r in reversed(range(SUBLANES)):
            b_r = None
            for q in range(n_q):
                j = SUBLANES * q + r
                if j >= CONV_WIDTH:
                    continue
                term = taps_ref[j, :, cols] * operands[q]
                b_r = term if b_r is None else b_r + term
            if r < SUBLANES - 1:
                b_r = b_r + pltpu.roll(jnp.where(last_row, prev[r + 1], partial[r + 1]), 1, axis=0)
            partial[r] = b_r
        prev = partial
        if k >= 0:
            out.append(partial[0])
    return jnp.concatenate(out, axis=0)


def _mixer_kernel(*refs, tile, sub, seg, has_hist, emit_v):
    it = iter(refs)
    x_ref = next(it)
    mod_ref = next(it)
    hist_ref = next(it) if has_hist else None
    vecs_ref = next(it)
    w_in_ref = next(it)
    b_in_ref = next(it)
    taps_ref = next(it)
    w_a_ref = next(it)
    ws_ref = next(it)
    bs_ref = next(it)
    w_b_ref = next(it)
    w_o_ref = next(it)
    w_r_ref = next(it)
    b_r_ref = next(it)
    x1_ref = next(it)
    h2_ref = next(it)
    route_ref = next(it)
    hist_out_ref = next(it)
    count_ref = next(it)
    v_out_ref = next(it) if emit_v else None
    count_acc_ref = next(it)
    carry_ref = None if has_hist else next(it)

    if not has_hist:
        @pl.when(pl.program_id(1) == 0)
        def _():
            carry_ref[...] = jnp.zeros_like(carry_ref)

    @pl.when((pl.program_id(0) == 0) & (pl.program_id(1) == 0))
    def _():
        count_acc_ref[...] = jnp.zeros_like(count_acc_ref)

    windows = []
    for k in range(tile // sub):
        rows = pl.ds(k * sub, sub)
        windows.append(_mixer_stages(
            x_ref.at[rows], mod_ref, hist_ref, vecs_ref, w_in_ref, b_in_ref, taps_ref,
            w_a_ref, ws_ref, bs_ref, w_b_ref, w_o_ref, w_r_ref, b_r_ref,
            x1_ref.at[rows], h2_ref.at[rows], route_ref.at[rows], hist_out_ref,
            v_out_ref.at[rows] if emit_v else None, count_acc_ref, carry_ref,
            tile=sub, seg=min(seg, sub)))
    live = [True] * len(windows)
    step = 0
    while any(live):
        for k, window in enumerate(windows):
            if live[k] and step >= k * MIXER_SKEW:
                live[k] = next(window, None) is not None
        step += 1
    count_ref[...] = count_acc_ref[...]


def _mixer_stages(x_ref, mod_ref, hist_ref, vecs_ref, w_in_ref, b_in_ref, taps_ref, w_a_ref, ws_ref,
                  bs_ref, w_b_ref, w_o_ref, w_r_ref, b_r_ref, x1_ref, h2_ref, route_ref,
                  hist_out_ref, v_out_ref, count_acc_ref, carry_ref, *, tile, seg):
    has_hist = hist_ref is not None
    emit_v = v_out_ref is not None
    d = x_ref.shape[-1]
    n_seg = tile // seg
    chunk = min(seg, GMLP_CHUNK)
    n_chunk = tile // chunk
    conv_cols = d // CONV_COL_BLOCKS

    vecs = vecs_ref[...]
    conv_b, ln_g, ln_b, v_ln_g, v_ln_b, g_norm1, g_norm2 = [vecs[i:i + 1] for i in range(7)]
    mod = lambda k: _segment_rows(mod_ref, k * d, (k + 1) * d, n_seg, seg)
    proj = lambda lo, hi: (jnp.dot(hb, w_in_ref[:, lo * d:hi * d], preferred_element_type=F32)
                           + b_in_ref[:, lo * d:hi * d])

    x = x_ref[...]
    hb = (_rms_norm(x, g_norm1) * (1.0 + mod(1)) + mod(0)).astype(BF16)
    yield True

    za = proj(0, 2)
    yield True
    a = za[:, :d] * _sigmoid(za[:, d:])
    a_ext = []
    for g in range(n_seg):
        a_seg = a[g * seg:(g + 1) * seg]
        hist = hist_ref[g] if has_hist else carry_ref[...]
        a_ext.append(jnp.concatenate([hist, a_seg], axis=0))
        hist_out_ref[g] = a_seg[seg - HIST_PAD:]
    if not has_hist:
        carry_ref[...] = a[tile - HIST_PAD:]

    def conv_block(c):
        cols = slice(c * conv_cols, (c + 1) * conv_cols)
        parts = [_causal_conv(e, taps_ref, cols, seg) for e in a_ext]
        return parts[0] if n_seg == 1 else jnp.concatenate(parts, axis=0)
    yield True

    zb = proj(2, 6)
    zu, zv, zg_a, zg_b = [zb[:, k * d:(k + 1) * d] for k in range(4)]
    conv_parts = []
    for c in range(CONV_COL_BLOCKS):
        conv_parts.append(conv_block(c))
        yield True
    conv = conv_parts[0] if CONV_COL_BLOCKS == 1 else jnp.concatenate(conv_parts, axis=1)

    u = _gelu_tanh(zu)
    yield True
    v = _layer_norm(_gelu_tanh(zv), v_ln_g, v_ln_b)
    if emit_v:
        v_out_ref[...] = v
    vb = v.astype(BF16)
    yield True
    gd = d // GMLP_GROUPS
    causal = (lax.broadcasted_iota(jnp.int32, (chunk, chunk), 0)
              >= lax.broadcasted_iota(jnp.int32, (chunk, chunk), 1))
    mixed_cols = []
    for hg in range(GMLP_GROUPS):
        w = jnp.where(causal, ws_ref[hg], 0.0).astype(BF16)
        rhs = [vb[c * chunk:(c + 1) * chunk, hg * gd:(hg + 1) * gd] for c in range(n_chunk)]
        rhs = rhs[0] if n_chunk == 1 else jnp.concatenate(rhs, axis=1)
        out = jnp.dot(w, rhs, preferred_element_type=F32)
        cols = [out[:, c * gd:(c + 1) * gd] for c in range(n_chunk)]
        mixed_cols.append(cols[0] if n_chunk == 1 else jnp.concatenate(cols, axis=0))
    mixed = jnp.concatenate(mixed_cols, axis=1)
    ca = _layer_norm(conv + conv_b, ln_g, ln_b)
    ca = ca * _sigmoid(ca)
    yield True
    y_a = jnp.dot(ca.astype(BF16), w_a_ref[...], preferred_element_type=F32)
    bs = bs_ref[...]
    mixed = mixed + (bs if n_chunk == 1 else jnp.concatenate([bs] * n_chunk, axis=0))
    gated = (u * mixed).astype(BF16)
    yield True
    y_b = jnp.dot(gated, w_b_ref[...], preferred_element_type=F32)
    gate_a = _sigmoid(zg_a)
    gate_b = _sigmoid(zg_b)
    yield True

    merged = gate_a * y_a + gate_b * y_b
    m = jnp.dot(merged.astype(BF16), w_o_ref[...], preferred_element_type=F32)
    yield True
    x1 = x + mod(2) * m
    x1_ref[...] = x1

    h2 = _rms_norm(x1, g_norm2) * (1.0 + mod(4)) + mod(3)
    h2_ref[:, 0:d] = h2.astype(h2_ref.dtype)
    logits = jnp.dot(h2.astype(BF16), w_r_ref[...], preferred_element_type=F32) + b_r_ref[...]
    yield True
    lane = lax.broadcasted_iota(jnp.int32, (tile, ROUTER_LANES), 1).astype(F32)
    is_group = lane < N_GROUPS
    lg = jnp.where(is_group, logits, NEG_BIG)
    mg = jnp.max(lg, axis=1, keepdims=True)
    gsel = jnp.min(jnp.where(lg == mg, lane, float(ROUTER_LANES)), axis=1, keepdims=True)
    pgsel = 1.0 / jnp.sum(jnp.where(is_group, jnp.exp(lg - mg), 0.0), axis=1, keepdims=True)
    lo = EXPERT_LANE0 + EXPERTS_PER_GROUP * gsel
    le = jnp.where((lane >= lo) & (lane < lo + EXPERTS_PER_GROUP), logits, NEG_BIG)
    m1 = jnp.max(le, axis=1, keepdims=True)
    i1 = jnp.min(jnp.where(le == m1, lane, float(ROUTER_LANES)), axis=1, keepdims=True)
    le2 = jnp.where(lane == i1, NEG_BIG, le)
    m2 = jnp.max(le2, axis=1, keepdims=True)
    i2 = jnp.min(jnp.where(le2 == m2, lane, float(ROUTER_LANES)), axis=1, keepdims=True)
    e2 = jnp.exp(m2 - m1)
    w1 = pgsel / (1.0 + e2)
    w2 = w1 * e2
    first_lower = i1 < i2
    ea = jnp.where(first_lower, i1, i2) - lo
    eb = jnp.where(first_lower, i2, i1) - lo
    bucket = gsel * PAIRS_PER_GROUP + ea * (2 * EXPERTS_PER_GROUP - 1 - ea) * 0.5 + (eb - ea - 1.0)
    wa = jnp.where(first_lower, w1, w2)
    wb = jnp.where(first_lower, w2, w1)
    route = (jnp.where(lane == i1, w1, 0.0) + jnp.where(lane == i2, w2, 0.0)
             + jnp.where(lane == BUCKET_LANE, bucket, 0.0)
             + jnp.where(lane == WA_LANE, wa, 0.0) + jnp.where(lane == WB_LANE, wb, 0.0))
    route_ref[...] = route
    if h2_ref.shape[-1] > d:
        h2_ref[:, d:d + ROUTER_LANES] = route

    count_acc_ref[...] += jnp.sum(jnp.where(lane == bucket, 1.0, 0.0), axis=0, keepdims=True)


def _const_spec(shape):
    nd = len(shape)
    return pl.BlockSpec(shape, lambda i, t: (0,) * nd, pipeline_mode=pl.Buffered(1))


def _mixer(x, mod, hist, wts, *, tile, sub, seg, emit_v, h2_dtype, h2_with_route):
    nb, s, d = x.shape
    n_seg = tile // seg
    has_hist = hist is not None
    grid = (nb, s // tile)
    chunk = min(seg, GMLP_CHUNK)

    in_specs = [pl.BlockSpec((None, tile, d), lambda i, t: (i, t, 0)),
                pl.BlockSpec((n_seg, 1, mod.shape[-1]), lambda i, t: (i, 0, 0))]
    args = [x, mod]
    if has_hist:
        in_specs.append(pl.BlockSpec((n_seg, HIST_PAD, d), lambda i, t: (i, 0, 0)))
        args.append(hist)
    consts = [wts["vecs"], wts["w_in"], wts["b_in"], wts["taps"], wts["w_a"],
              wts["w_s"][:, :chunk, :chunk], wts["b_s"][:chunk], wts["w_b"], wts["w_o"],
              wts["w_r"], wts["b_r"]]
    in_specs += [_const_spec(c.shape) for c in consts]
    args += consts

    h2_width = d + ROUTER_LANES if h2_with_route else d
    out_shape = [jax.ShapeDtypeStruct((nb, s, d), F32),
                 jax.ShapeDtypeStruct((nb, s, h2_width), h2_dtype),
                 jax.ShapeDtypeStruct((nb, s, ROUTER_LANES), F32),
                 jax.ShapeDtypeStruct((nb * n_seg, HIST_PAD, d), F32),
                 jax.ShapeDtypeStruct((SUBLANES, ROUTER_LANES), F32)]
    out_specs = [pl.BlockSpec((None, tile, d), lambda i, t: (i, t, 0)),
                 pl.BlockSpec((None, tile, h2_width), lambda i, t: (i, t, 0)),
                 pl.BlockSpec((None, tile, ROUTER_LANES), lambda i, t: (i, t, 0)),
                 pl.BlockSpec((n_seg, HIST_PAD, d), lambda i, t: (i, 0, 0)),
                 pl.BlockSpec((SUBLANES, ROUTER_LANES), lambda i, t: (0, 0))]
    if emit_v:
        out_shape.append(jax.ShapeDtypeStruct((nb, s, d), F32))
        out_specs.append(pl.BlockSpec((None, tile, d), lambda i, t: (i, t, 0)))
    scratch = [pltpu.VMEM((SUBLANES, ROUTER_LANES), F32)]
    if not has_hist:
        scratch.append(pltpu.VMEM((HIST_PAD, d), F32))

    return pl.pallas_call(
        functools.partial(_mixer_kernel, tile=tile, sub=sub, seg=seg, has_hist=has_hist,
                          emit_v=emit_v),
        grid=grid, in_specs=in_specs, out_specs=out_specs, out_shape=out_shape,
        scratch_shapes=scratch,
        compiler_params=pltpu.CompilerParams(
            dimension_semantics=("arbitrary", "arbitrary"),
            vmem_limit_bytes=VMEM_LIMIT_BYTES),
    )(*args)


def _moe_kernel(h2_ref, comb_ref, x1_ref, mod_ref, modf_ref, gfin_ref, wg_ref, wu_ref, wd_ref,
                y_ref, acc_ref, *, tile, seg):
    e = pl.program_id(2)
    d = x1_ref.shape[-1]
    n_seg = tile // seg

    @pl.when(e == 0)
    def _():
        acc_ref[...] = jnp.zeros_like(acc_ref)

    hb = h2_ref[...]
    hg = jnp.dot(hb, wg_ref[...], preferred_element_type=F32)
    hu = jnp.dot(hb, wu_ref[...], preferred_element_type=F32)
    he = (hg * _sigmoid(hg)) * hu
    ye = jnp.dot(he.astype(BF16), wd_ref[...], preferred_element_type=F32)
    lane = lax.broadcasted_iota(jnp.int32, (tile, ROUTER_LANES), 1)
    cw = jnp.sum(jnp.where(lane == e + EXPERT_LANE0, comb_ref[...], 0.0), axis=1, keepdims=True)
    acc_ref[...] += cw * ye

    @pl.when(e == N_EXPERTS - 1)
    def _():
        g2 = _segment_rows(mod_ref, 5 * d, 6 * d, n_seg, seg)
        shf = _segment_rows(modf_ref, 0, d, n_seg, seg)
        scf = _segment_rows(modf_ref, d, 2 * d, n_seg, seg)
        x2 = x1_ref[...] + g2 * acc_ref[...]
        y_ref[...] = _rms_norm(x2, gfin_ref[...]) * (1.0 + scf) + shf


def _moe(h2, comb, x1, mod, modf, wts, *, tile, seg):
    nb, s, d = x1.shape
    n_seg = tile // seg
    de = wts["w_gate"].shape[-1]
    tok = lambda i, t, e: (i, t, 0)
    per_seq = lambda i, t, e: (i, 0, 0)
    return pl.pallas_call(
        functools.partial(_moe_kernel, tile=tile, seg=seg),
        grid=(nb, s // tile, N_EXPERTS),
        in_specs=[pl.BlockSpec((None, tile, d), tok),
                  pl.BlockSpec((None, tile, ROUTER_LANES), tok),
                  pl.BlockSpec((None, tile, d), tok),
                  pl.BlockSpec((n_seg, 1, mod.shape[-1]), per_seq),
                  pl.BlockSpec((n_seg, 1, modf.shape[-1]), per_seq),
                  pl.BlockSpec((1, d), lambda i, t, e: (0, 0)),
                  pl.BlockSpec((None, d, de), lambda i, t, e: (e, 0, 0)),
                  pl.BlockSpec((None, d, de), lambda i, t, e: (e, 0, 0)),
                  pl.BlockSpec((None, de, d), lambda i, t, e: (e, 0, 0))],
        out_specs=pl.BlockSpec((None, tile, d), tok),
        out_shape=jax.ShapeDtypeStruct((nb, s, d), F32),
        scratch_shapes=[pltpu.VMEM((tile, d), F32)],
        compiler_params=pltpu.CompilerParams(
            dimension_semantics=("arbitrary", "arbitrary", "arbitrary"),
            vmem_limit_bytes=VMEM_LIMIT_BYTES),
    )(h2, comb, x1, mod, modf, wts["g_final"], wts["w_gate"], wts["w_up"], wts["w_down"])


def _plan_kernel(route_ref, start_ref, slot_ref, next_ref, *, tile):
    @pl.when(pl.program_id(0) == 0)
    def _():
        next_ref[...] = start_ref[...]

    bucket_row = route_ref[...].T[BUCKET_LANE:BUCKET_LANE + 1, :]
    rows = lax.broadcasted_iota(jnp.int32, (ROUTER_LANES, tile), 0).astype(F32)
    member = rows == bucket_row
    upper = (lax.broadcasted_iota(jnp.int32, (tile, tile), 0)
             <= lax.broadcasted_iota(jnp.int32, (tile, tile), 1))
    seen = jnp.dot(jnp.where(member, 1.0, 0.0).astype(BF16), jnp.where(upper, 1.0, 0.0).astype(BF16),
                   preferred_element_type=F32)
    nxt = next_ref[...]
    slot = jnp.sum(jnp.where(member, seen - 1.0 + nxt[:, 0:1], 0.0), axis=0, keepdims=True)
    slot_ref[...] = slot.astype(jnp.int32)
    next_ref[...] = nxt + seen[:, tile - 1:tile]


def _plan(route, bucket_start, *, tile):
    n = route.shape[0]
    start = jnp.broadcast_to(bucket_start[:, None], (ROUTER_LANES, ROUTER_LANES))
    slots = pl.pallas_call(
        functools.partial(_plan_kernel, tile=tile),
        grid=(n // tile,),
        in_specs=[pl.BlockSpec((tile, ROUTER_LANES), lambda t: (t, 0)),
                  pl.BlockSpec((ROUTER_LANES, ROUTER_LANES), lambda t: (0, 0))],
        out_specs=pl.BlockSpec((None, 1, tile), lambda t: (t, 0, 0)),
        out_shape=jax.ShapeDtypeStruct((n // tile, 1, tile), jnp.int32),
        scratch_shapes=[pltpu.VMEM((ROUTER_LANES, ROUTER_LANES), F32)],
        compiler_params=pltpu.CompilerParams(dimension_semantics=("arbitrary",)),
    )(route, start)
    return slots.reshape(n)


def _dispatch_kernel(slot_ref, pad_tile_ref, src_ref, dst_hbm, zeros_ref, sem, *, tile, pad_rows):
    t = pl.program_id(0)

    @pl.when(t == 0)
    def _():
        zeros_ref[...] = jnp.zeros_like(zeros_ref)
        for phase in ("start", "wait"):
            for b in range(2 * N_BUCKETS):
                @pl.when(pad_tile_ref[b] >= 0)
                def _():
                    first_row = pl.multiple_of(pad_tile_ref[b], pad_rows)
                    copy = pltpu.make_async_copy(
                        zeros_ref, dst_hbm.at[pl.ds(first_row, pad_rows)], sem)
                    copy.start() if phase == "start" else copy.wait()

    base = t * tile
    for r in range(tile):
        pltpu.make_async_copy(src_ref.at[pl.ds(r, 1)], dst_hbm.at[pl.ds(slot_ref[base + r], 1)],
                              sem).start()
    pltpu.make_async_copy(src_ref, dst_hbm.at[pl.ds(0, tile)], sem).wait()


def _dispatch(slots, pad_tile, src, n_slots, *, tile, pad_rows):
    n, d = src.shape
    return pl.pallas_call(
        functools.partial(_dispatch_kernel, tile=tile, pad_rows=pad_rows),
        grid_spec=pltpu.PrefetchScalarGridSpec(
            num_scalar_prefetch=2, grid=(n // tile,),
            in_specs=[pl.BlockSpec((tile, d), lambda t, s, p: (t, 0))],
            out_specs=pl.BlockSpec(memory_space=pl.ANY),
            scratch_shapes=[pltpu.VMEM((pad_rows, d), src.dtype), pltpu.SemaphoreType.DMA(())]),
        out_shape=jax.ShapeDtypeStruct((n_slots, d), src.dtype),
        compiler_params=pltpu.CompilerParams(dimension_semantics=("arbitrary",),
                                             vmem_limit_bytes=VMEM_LIMIT_BYTES),
    )(slots, pad_tile, src)


def _grouped_moe_kernel(ea_ref, eb_ref, n_used_ref, hs_ref, wga_ref, wua_ref, wda_ref,
                        wgb_ref, wub_ref, wdb_ref, ys_ref):
    del ea_ref, eb_ref
    d = ys_ref.shape[-1]
    j = pl.program_id(0)

    @pl.when(j < n_used_ref[0])
    def _():
        xb = hs_ref[:, 0:d].astype(BF16)
        record = hs_ref[:, d:d + ROUTER_LANES]
        acc = None
        for lane, (wg, wu, wd) in ((WA_LANE, (wga_ref, wua_ref, wda_ref)),
                                   (WB_LANE, (wgb_ref, wub_ref, wdb_ref))):
            g = jnp.dot(xb, wg[...], preferred_element_type=F32)
            u = jnp.dot(xb, wu[...], preferred_element_type=F32)
            he = (g * _sigmoid(g)) * u
            ye = jnp.dot(he.astype(BF16), wd[...], preferred_element_type=F32)
            term = record[:, lane:lane + 1] * ye
            acc = term if acc is None else acc + term
        ys_ref[...] = acc

    @pl.when(j >= n_used_ref[0])
    def _():
        ys_ref[...] = jnp.zeros_like(ys_ref)


def _grouped_moe(hs, tile_ea, tile_eb, n_used, wts, *, tile):
    n_slots, width = hs.shape
    d = width - ROUTER_LANES
    de = wts["w_gate"].shape[-1]
    lower = lambda j, ea, eb, nu: (ea[j], 0, 0)
    higher = lambda j, ea, eb, nu: (eb[j], 0, 0)
    w_specs = lambda idx: [pl.BlockSpec((None, d, de), idx), pl.BlockSpec((None, d, de), idx),
                           pl.BlockSpec((None, de, d), idx)]
    w_args = [wts["w_gate"], wts["w_up"], wts["w_down"]]
    return pl.pallas_call(
        _grouped_moe_kernel,
        grid_spec=pltpu.PrefetchScalarGridSpec(
            num_scalar_prefetch=3, grid=(n_slots // tile,),
            in_specs=[pl.BlockSpec((tile, width),
                                   lambda j, ea, eb, nu: (jnp.minimum(j, nu[0] - 1), 0))]
                     + w_specs(lower) + w_specs(higher),
            out_specs=pl.BlockSpec((tile, d), lambda j, ea, eb, nu: (j, 0))),
        out_shape=jax.ShapeDtypeStruct((n_slots, d), F32),
        compiler_params=pltpu.CompilerParams(dimension_semantics=("arbitrary",),
                                             vmem_limit_bytes=VMEM_LIMIT_BYTES),
    )(tile_ea, tile_eb, n_used, hs, *w_args, *w_args)


def _combine_kernel(slot_ref, x1_ref, mod_ref, modf_ref, gfin_ref, ys_hbm, y_ref,
                    ybuf, sems, *, tile, n_steps):
    t = pl.program_id(0)
    d = x1_ref.shape[-1]

    def fetch(step, buf):
        for r in range(tile):
            pltpu.make_async_copy(ys_hbm.at[pl.ds(slot_ref[step * tile + r], 1)],
                                  ybuf.at[buf, pl.ds(r, 1)], sems.at[buf]).start()

    def wait(buf):
        pltpu.make_async_copy(ys_hbm.at[pl.ds(0, tile)], ybuf.at[buf], sems.at[buf]).wait()

    @pl.when(t == 0)
    def _():
        fetch(0, 0)

    def step(buf):
        fetch((t + 1) % n_steps, 1 - buf)
        wait(buf)
        x2 = x1_ref[...] + mod_ref[0, :, 5 * d:6 * d] * ybuf[buf]
        y_ref[...] = (_rms_norm(x2, gfin_ref[...]) * (1.0 + modf_ref[0, :, d:2 * d])
                      + modf_ref[0, :, 0:d])

    for buf in range(2):
        pl.when(t % 2 == buf)(functools.partial(step, buf))

    @pl.when(t == n_steps - 1)
    def _():
        wait(n_steps % 2)


def _combine(slots, x1, mod, modf, ys, wts, *, tile, tiles_per_seq):
    n, d = x1.shape
    n_steps = n // tile
    tok = lambda t, s: (t, 0)
    per_seq = lambda t, s: (t // tiles_per_seq, 0, 0)
    return pl.pallas_call(
        functools.partial(_combine_kernel, tile=tile, n_steps=n_steps),
        grid_spec=pltpu.PrefetchScalarGridSpec(
            num_scalar_prefetch=1, grid=(n_steps,),
            in_specs=[pl.BlockSpec((tile, d), tok),
                      pl.BlockSpec((1, 1, mod.shape[-1]), per_seq),
                      pl.BlockSpec((1, 1, modf.shape[-1]), per_seq),
                      pl.BlockSpec((1, d), lambda t, s: (0, 0)),
                      pl.BlockSpec(memory_space=pl.ANY)],
            out_specs=pl.BlockSpec((tile, d), tok),
            scratch_shapes=[pltpu.VMEM((2, tile, d), F32), pltpu.SemaphoreType.DMA((2,))]),
        out_shape=jax.ShapeDtypeStruct((n, d), F32),
        compiler_params=pltpu.CompilerParams(dimension_semantics=("arbitrary",),
                                             vmem_limit_bytes=VMEM_LIMIT_BYTES),
    )(slots, x1, mod, modf, wts["g_final"], ys)


def _sorted_moe(h2, x1, route, counts, mod, modf, wts, *, seq_len):
    n, d = x1.shape
    tile = SORT_TILE
    n_tiles = n // tile + N_BUCKETS
    cnt = counts[:N_BUCKETS].astype(jnp.int32)
    padded = (cnt + tile - 1) // tile * tile
    ends = jnp.cumsum(padded)
    starts = ends - padded
    n_used = ends[-1] // tile
    tile_start = jnp.arange(n_tiles, dtype=jnp.int32) * tile
    tile_bucket = jnp.sum((ends[None, :] <= tile_start[:, None]).astype(jnp.int32), axis=1)
    last_bucket = tile_bucket[n_used - 1]
    tile_bucket = jnp.where(jnp.arange(n_tiles) < n_used, tile_bucket, last_bucket)
    experts = jnp.asarray(_BUCKET_EXPERTS)[tile_bucket]
    bucket_start = jnp.zeros((ROUTER_LANES,), F32).at[:N_BUCKETS].set(starts.astype(F32))

    tail_tile = n_used + jnp.arange(N_BUCKETS, dtype=jnp.int32)
    pad_tile = jnp.concatenate([jnp.where(padded > cnt, ends - tile, -1),
                                jnp.where(tail_tile < n_tiles, tail_tile * tile, -1)])

    slots = _plan(route, bucket_start, tile=min(PLAN_TILE, n))
    hs = _dispatch(slots, pad_tile, h2, n_tiles * tile, tile=min(DISPATCH_TILE, n), pad_rows=tile)
    ys = _grouped_moe(hs, experts[:, 0], experts[:, 1], n_used.reshape(1), wts, tile=tile)
    return _combine(slots, x1, mod, modf, ys, wts, tile=COMBINE_TILE,
                    tiles_per_seq=seq_len // COMBINE_TILE)


def _layer_weights(l,w_in, b_in, conv_w, conv_b, ln_g, ln_b, w_a, v_ln_g, v_ln_b, w_s, b_s,
                   w_b, w_o, g_norm1, g_norm2, w_rg, b_rg, w_re, b_re, w_gate, w_up, w_down,
                   g_final):
    d = w_in.shape[1]
    gd = d // GMLP_GROUPS
    zero_row = jnp.zeros((1, d), F32)
    vecs = jnp.stack([conv_b[l], ln_g[l], ln_b[l], v_ln_g[l], v_ln_b[l], g_norm1[l], g_norm2[l],
                      zero_row[0]])
    taps = jnp.concatenate([conv_w[l][::-1], jnp.zeros((CONV_TAPS_PAD - CONV_WIDTH, d), F32)])
    taps = jnp.broadcast_to(taps[:, None, :], (CONV_TAPS_PAD, SUBLANES, d))
    pad = ROUTER_LANES - N_GROUPS - N_EXPERTS
    w_r = jnp.concatenate([w_rg[l], w_re[l], jnp.zeros((d, pad), F32)], axis=1)
    b_r = jnp.concatenate([b_rg[l], b_re[l], jnp.zeros((pad,), F32)]).reshape(1, ROUTER_LANES)
    return dict(
        vecs=vecs, taps=taps,
        w_in=w_in[l].astype(BF16), b_in=b_in[l].reshape(1, -1),
        w_a=w_a[l].astype(BF16), w_b=w_b[l].astype(BF16), w_o=w_o[l].astype(BF16),
        w_s=w_s[l], b_s=jnp.repeat(b_s[l].T, gd, axis=1),
        w_r=w_r.astype(BF16), b_r=b_r,
        w_gate=w_gate[l].astype(BF16), w_up=w_up[l].astype(BF16), w_down=w_down[l].astype(BF16),
        g_final=g_final.reshape(1, d))


def kernel(x_prompt, x_sample, cache_conv, c_prompt, c_sample, w_ada, b_ada, g_norm1, g_norm2, w_in, b_in, conv_w, conv_b, ln_g, ln_b, w_a, v_ln_g, v_ln_b, w_s, b_s, w_b, w_o, w_rg, b_rg, w_re, b_re, w_gate, w_up, w_down, g_final, w_ada_f, b_ada_f):
    depth = w_in.shape[0]
    assert depth == 1, "the streaming step is written for a single layer"
    l = 0
    batch, seq, d = x_prompt.shape
    dec_batch, dec_seq, _ = x_sample.shape
    assert dec_seq == HIST_PAD and seq % PROMPT_TILE == 0 and (dec_batch * dec_seq) % MIXER_TILE == 0
    assert (batch * seq) % min(DISPATCH_TILE, batch * seq) == 0 and seq % COMBINE_TILE == 0

    wts = _layer_weights(l, w_in, b_in, conv_w, conv_b, ln_g, ln_b, w_a, v_ln_g, v_ln_b, w_s, b_s,
                         w_b, w_o, g_norm1, g_norm2, w_rg, b_rg, w_re, b_re, w_gate, w_up, w_down,
                         g_final)

    c_all = jnp.concatenate([c_prompt, c_sample], axis=0)
    mod = _modulation(c_all, w_ada[l], b_ada[l])[:, None, :]
    modf = _modulation(c_all, w_ada_f, b_ada_f)[:, None, :]

    x1p, h2p, routep, histp, counts = _mixer(x_prompt, mod[:batch], None, wts, tile=PROMPT_TILE,
                                             sub=MIXER_TILE, seg=PROMPT_TILE, emit_v=False,
                                             h2_dtype=F32, h2_with_route=True)
    n_prompt = batch * seq
    y_prompt = _sorted_moe(h2p.reshape(n_prompt, d + ROUTER_LANES), x1p.reshape(n_prompt, d),
                           routep.reshape(n_prompt, ROUTER_LANES), counts[0], mod[:batch],
                           modf[:batch], wts, seq_len=seq).reshape(batch, seq, d)

    n_tok = dec_batch * dec_seq
    xs = x_sample.reshape(n_tok // MIXER_TILE, MIXER_TILE, d)
    hist_s = jnp.pad(cache_conv[l], ((0, 0), (HIST_PAD - HIST, 0), (0, 0)))
    x1s, h2s, combs, hists, _, vs = _mixer(xs, mod[batch:], hist_s, wts, tile=MIXER_TILE,
                                           sub=MIXER_TILE, seg=dec_seq, emit_v=True, h2_dtype=BF16,
                                           h2_with_route=False)
    moe_tile_s = min(MOE_TILE, n_tok)
    y_sample = _moe(h2s.reshape(n_tok // moe_tile_s, moe_tile_s, d),
                    combs.reshape(n_tok // moe_tile_s, moe_tile_s, ROUTER_LANES),
                    x1s.reshape(n_tok // moe_tile_s, moe_tile_s, d),
                    mod[batch:], modf[batch:], wts, tile=moe_tile_s, seg=dec_seq)

    state_conv_prompt = histp[None, :, HIST_PAD - HIST:, :]
    state_conv_sample = hists[None, :, HIST_PAD - HIST:, :]
    state_gmlp_v_sample = vs.reshape(1, dec_batch, dec_seq, d)
    return (y_prompt, y_sample.reshape(dec_batch, dec_seq, d), state_conv_prompt,
            state_conv_sample, state_gmlp_v_sample)
```

```python
import functools

import jax
import jax.numpy as jnp
import numpy as np
from jax import lax
from jax.experimental import pallas as pl
from jax.experimental.pallas import tpu as pltpu

F32 = jnp.float32
BF16 = jnp.bfloat16

EPS = 1e-6
CONV_WIDTH = 31
HIST = CONV_WIDTH - 1
SUBLANES = 8
LANES = 128
HIST_PAD = 32
CONV_TAPS_PAD = 32
GMLP_CHUNK = 128
GMLP_GROUPS = 8
N_GROUPS = 4
EXPERTS_PER_GROUP = 8
N_EXPERTS = N_GROUPS * EXPERTS_PER_GROUP
ROUTER_LANES = LANES
EXPERT_LANE0 = N_GROUPS
NEG_BIG = -1e30
VMEM_LIMIT_BYTES = 56 * 1024 * 1024

MIXER_TILE = 256
PROMPT_TILE = 512
CONV_COL_BLOCKS = 4
MIXER_SKEW = 1
MOE_TILE = 1024
SORT_TILE = 256
PLAN_TILE = 1024
DISPATCH_TILE = 2048
COMBINE_TILE = 1024

PAIRS_PER_GROUP = EXPERTS_PER_GROUP * (EXPERTS_PER_GROUP - 1) // 2
N_BUCKETS = N_GROUPS * PAIRS_PER_GROUP
BUCKET_LANE = 64
WA_LANE = 65
WB_LANE = 66
_BUCKET_EXPERTS = np.array([(g * EXPERTS_PER_GROUP + a, g * EXPERTS_PER_GROUP + b)
                            for g in range(N_GROUPS)
                            for a in range(EXPERTS_PER_GROUP)
                            for b in range(a + 1, EXPERTS_PER_GROUP)], dtype=np.int32)


def _sigmoid(x):
    return 1.0 / (1.0 + jnp.exp2(x * -1.4426950408889634))


def _gelu_tanh(x):
    c = 0.7978845608028654
    inner = x * (c + (c * 0.044715) * (x * x))
    return (0.5 * x) * (1.0 + jnp.tanh(inner))


def _rms_norm(x, g):
    return (x * lax.rsqrt(jnp.mean(x * x, axis=-1, keepdims=True) + EPS)) * g


def _layer_norm(x, g, b):
    mu = jnp.mean(x, axis=-1, keepdims=True)
    xc = x - mu
    var = jnp.mean(xc * xc, axis=-1, keepdims=True)
    return xc * lax.rsqrt(var + EPS) * g + b


def _segment_rows(ref, lo, hi, n_seg, seg):
    if n_seg == 1:
        return ref[0, :, lo:hi]
    parts = [jnp.broadcast_to(ref[g, :, lo:hi], (seg, hi - lo)) for g in range(n_seg)]
    return jnp.concatenate(parts, axis=0)


def _modulation_kernel(c_ref, w_ref, b_ref, o_ref):
    c = c_ref[...]
    s = c * _sigmoid(c)
    o_ref[...] = jnp.dot(s, w_ref[...], precision=lax.Precision.HIGHEST,
                         preferred_element_type=F32) + b_ref[...]


def _modulation(c, w, b):
    n_seq, d = c.shape
    n_out = w.shape[1]
    bn = 1024
    return pl.pallas_call(
        _modulation_kernel,
        grid=(n_out // bn,),
        in_specs=[pl.BlockSpec((n_seq, d), lambda j: (0, 0)),
                  pl.BlockSpec((d, bn), lambda j: (0, j)),
                  pl.BlockSpec((1, bn), lambda j: (0, j))],
        out_specs=pl.BlockSpec((n_seq, bn), lambda j: (0, j)),
        out_shape=jax.ShapeDtypeStruct((n_seq, n_out), F32),
    )(c, w, b.reshape(1, n_out))


def _causal_conv(a_ext, taps_ref, cols, seg):
    n_q = CONV_TAPS_PAD // SUBLANES
    n_hist = HIST_PAD // SUBLANES
    group = lambda m: a_ext[SUBLANES * m:SUBLANES * (m + 1), cols]
    width = group(0).shape[-1]
    last_row = lax.broadcasted_iota(jnp.int32, (SUBLANES, width), 0) == SUBLANES - 1
    prev = [jnp.zeros((SUBLANES, width), F32)] * SUBLANES
    out = []
    for k in range(-1, seg // SUBLANES):
        operands = [group(n_hist + k - q) for q in range(n_q)]
        partial = [None] * SUBLANES
        for r in reversed(range(SUBLANES)):
            b_r = None
            for q in range(n_q):
                j = SUBLANES * q + r
                if j >= CONV_WIDTH:
                    continue
                term = taps_ref[j, :, cols] * operands[q]
                b_r = term if b_r is None else b_r + term
            if r < SUBLANES - 1:
                b_r = b_r + pltpu.roll(jnp.where(last_row, prev[r + 1], partial[r + 1]), 1, axis=0)
            partial[r] = b_r
        prev = partial
        if k >= 0:
            out.append(partial[0])
    return jnp.concatenate(out, axis=0)


def _mixer_kernel(*refs, tile, sub, seg, has_hist, emit_v):
    it = iter(refs)
    x_ref = next(it)
    mod_ref = next(it)
    hist_ref = next(it) if has_hist else None
    vecs_ref = next(it)
    w_in_ref = next(it)
    b_in_ref = next(it)
    taps_ref = next(it)
    w_a_ref = next(it)
    ws_ref = next(it)
    bs_ref = next(it)
    w_b_ref = next(it)
    w_o_ref = next(it)
    w_r_ref = next(it)
    b_r_ref = next(it)
    x1_ref = next(it)
    h2_ref = next(it)
    route_ref = next(it)
    hist_out_ref = next(it)
    count_ref = next(it)
    v_out_ref = next(it) if emit_v else None
    count_acc_ref = next(it)
    carry_ref = None if has_hist else next(it)

    if not has_hist:
        @pl.when(pl.program_id(1) == 0)
        def _():
            carry_ref[...] = jnp.zeros_like(carry_ref)

    @pl.when((pl.program_id(0) == 0) & (pl.program_id(1) == 0))
    def _():
        count_acc_ref[...] = jnp.zeros_like(count_acc_ref)

    windows = []
    for k in range(tile // sub):
        rows = pl.ds(k * sub, sub)
        windows.append(_mixer_stages(
            x_ref.at[rows], mod_ref, hist_ref, vecs_ref, w_in_ref, b_in_ref, taps_ref,
            w_a_ref, ws_ref, bs_ref, w_b_ref, w_o_ref, w_r_ref, b_r_ref,
            x1_ref.at[rows], h2_ref.at[rows], route_ref.at[rows], hist_out_ref,
            v_out_ref.at[rows] if emit_v else None, count_acc_ref, carry_ref,
            tile=sub, seg=min(seg, sub)))
    live = [True] * len(windows)
    step = 0
    while any(live):
        for k, window in enumerate(windows):
            if live[k] and step >= k * MIXER_SKEW:
                live[k] = next(window, None) is not None
        step += 1
    count_ref[...] = count_acc_ref[...]


def _mixer_stages(x_ref, mod_ref, hist_ref, vecs_ref, w_in_ref, b_in_ref, taps_ref, w_a_ref, ws_ref,
                  bs_ref, w_b_ref, w_o_ref, w_r_ref, b_r_ref, x1_ref, h2_ref, route_ref,
                  hist_out_ref, v_out_ref, count_acc_ref, carry_ref, *, tile, seg):
    has_hist = hist_ref is not None
    emit_v = v_out_ref is not None
    d = x_ref.shape[-1]
    n_seg = tile // seg
    chunk = min(seg, GMLP_CHUNK)
    n_chunk = tile // chunk
    conv_cols = d // CONV_COL_BLOCKS

    vecs = vecs_ref[...]
    conv_b, ln_g, ln_b, v_ln_g, v_ln_b, g_norm1, g_norm2 = [vecs[i:i + 1] for i in range(7)]
    mod = lambda k: _segment_rows(mod_ref, k * d, (k + 1) * d, n_seg, seg)
    proj = lambda lo, hi: (jnp.dot(hb, w_in_ref[:, lo * d:hi * d], preferred_element_type=F32)
                           + b_in_ref[:, lo * d:hi * d])

    x = x_ref[...]
    hb = (_rms_norm(x, g_norm1) * (1.0 + mod(1)) + mod(0)).astype(BF16)
    yield True

    za = proj(0, 2)
    yield True
    a = za[:, :d] * _sigmoid(za[:, d:])
    a_ext = []
    for g in range(n_seg):
        a_seg = a[g * seg:(g + 1) * seg]
        hist = hist_ref[g] if has_hist else carry_ref[...]
        a_ext.append(jnp.concatenate([hist, a_seg], axis=0))
        hist_out_ref[g] = a_seg[seg - HIST_PAD:]
    if not has_hist:
        carry_ref[...] = a[tile - HIST_PAD:]

    def conv_block(c):
        cols = slice(c * conv_cols, (c + 1) * conv_cols)
        parts = [_causal_conv(e, taps_ref, cols, seg) for e in a_ext]
        return parts[0] if n_seg == 1 else jnp.concatenate(parts, axis=0)
    yield True

    zb = proj(2, 6)
    zu, zv, zg_a, zg_b = [zb[:, k * d:(k + 1) * d] for k in range(4)]
    conv_parts = []
    for c in range(CONV_COL_BLOCKS):
        conv_parts.append(conv_block(c))
        yield True
    conv = conv_parts[0] if CONV_COL_BLOCKS == 1 else jnp.concatenate(conv_parts, axis=1)

    u = _gelu_tanh(zu)
    yield True
    v = _layer_norm(_gelu_tanh(zv), v_ln_g, v_ln_b)
    if emit_v:
        v_out_ref[...] = v
    vb = v.astype(BF16)
    yield True
    gd = d // GMLP_GROUPS
    causal = (lax.broadcasted_iota(jnp.int32, (chunk, chunk), 0)
              >= lax.broadcasted_iota(jnp.int32, (chunk, chunk), 1))
    mixed_cols = []
    for hg in range(GMLP_GROUPS):
        w = jnp.where(causal, ws_ref[hg], 0.0).astype(BF16)
        rhs = [vb[c * chunk:(c + 1) * chunk, hg * gd:(hg + 1) * gd] for c in range(n_chunk)]
        rhs = rhs[0] if n_chunk == 1 else jnp.concatenate(rhs, axis=1)
        out = jnp.dot(w, rhs, preferred_element_type=F32)
        cols = [out[:, c * gd:(c + 1) * gd] for c in range(n_chunk)]
        mixed_cols.append(cols[0] if n_chunk == 1 else jnp.concatenate(cols, axis=0))
    mixed = jnp.concatenate(mixed_cols, axis=1)
    ca = _layer_norm(conv + conv_b, ln_g, ln_b)
    ca = ca * _sigmoid(ca)
    yield True
    y_a = jnp.dot(ca.astype(BF16), w_a_ref[...], preferred_element_type=F32)
    bs = bs_ref[...]
    mixed = mixed + (bs if n_chunk == 1 else jnp.concatenate([bs] * n_chunk, axis=0))
    gated = (u * mixed).astype(BF16)
    yield True
    y_b = jnp.dot(gated, w_b_ref[...], preferred_element_type=F32)
    gate_a = _sigmoid(zg_a)
    gate_b = _sigmoid(zg_b)
    yield True

    merged = gate_a * y_a + gate_b * y_b
    m = jnp.dot(merged.astype(BF16), w_o_ref[...], preferred_element_type=F32)
    yield True
    x1 = x + mod(2) * m
    x1_ref[...] = x1

    h2 = _rms_norm(x1, g_norm2) * (1.0 + mod(4)) + mod(3)
    h2_ref[:, 0:d] = h2.astype(h2_ref.dtype)
    logits = jnp.dot(h2.astype(BF16), w_r_ref[...], preferred_element_type=F32) + b_r_ref[...]
    yield True
    lane = lax.broadcasted_iota(jnp.int32, (tile, ROUTER_LANES), 1).astype(F32)
    is_group = lane < N_GROUPS
    lg = jnp.where(is_group, logits, NEG_BIG)
    mg = jnp.max(lg, axis=1, keepdims=True)
    gsel = jnp.min(jnp.where(lg == mg, lane, float(ROUTER_LANES)), axis=1, keepdims=True)
    pgsel = 1.0 / jnp.sum(jnp.where(is_group, jnp.exp(lg - mg), 0.0), axis=1, keepdims=True)
    lo = EXPERT_LANE0 + EXPERTS_PER_GROUP * gsel
    le = jnp.where((lane >= lo) & (lane < lo + EXPERTS_PER_GROUP), logits, NEG_BIG)
    m1 = jnp.max(le, axis=1, keepdims=True)
    i1 = jnp.min(jnp.where(le == m1, lane, float(ROUTER_LANES)), axis=1, keepdims=True)
    le2 = jnp.where(lane == i1, NEG_BIG, le)
    m2 = jnp.max(le2, axis=1, keepdims=True)
    i2 = jnp.min(jnp.where(le2 == m2, lane, float(ROUTER_LANES)), axis=1, keepdims=True)
    e2 = jnp.exp(m2 - m1)
    w1 = pgsel / (1.0 + e2)
    w2 = w1 * e2
    first_lower = i1 < i2
    ea = jnp.where(first_lower, i1, i2) - lo
    eb = jnp.where(first_lower, i2, i1) - lo
    bucket = gsel * PAIRS_PER_GROUP + ea * (2 * EXPERTS_PER_GROUP - 1 - ea) * 0.5 + (eb - ea - 1.0)
    wa = jnp.where(first_lower, w1, w2)
    wb = jnp.where(first_lower, w2, w1)
    route = (jnp.where(lane == i1, w1, 0.0) + jnp.where(lane == i2, w2, 0.0)
             + jnp.where(lane == BUCKET_LANE, bucket, 0.0)
             + jnp.where(lane == WA_LANE, wa, 0.0) + jnp.where(lane == WB_LANE, wb, 0.0))
    route_ref[...] = route
    if h2_ref.shape[-1] > d:
        h2_ref[:, d:d + ROUTER_LANES] = route

    count_acc_ref[...] += jnp.sum(jnp.where(lane == bucket, 1.0, 0.0), axis=0, keepdims=True)


def _const_spec(shape):
    nd = len(shape)
    return pl.BlockSpec(shape, lambda i, t: (0,) * nd, pipeline_mode=pl.Buffered(1))


def _mixer(x, mod, hist, wts, *, tile, sub, seg, emit_v, h2_dtype, h2_with_route):
    nb, s, d = x.shape
    n_seg = tile // seg
    has_hist = hist is not None
    grid = (nb, s // tile)
    chunk = min(seg, GMLP_CHUNK)

    in_specs = [pl.BlockSpec((None, tile, d), lambda i, t: (i, t, 0)),
                pl.BlockSpec((n_seg, 1, mod.shape[-1]), lambda i, t: (i, 0, 0))]
    args = [x, mod]
    if has_hist:
        in_specs.append(pl.BlockSpec((n_seg, HIST_PAD, d), lambda i, t: (i, 0, 0)))
        args.append(hist)
    consts = [wts["vecs"], wts["w_in"], wts["b_in"], wts["taps"], wts["w_a"],
              wts["w_s"][:, :chunk, :chunk], wts["b_s"][:chunk], wts["w_b"], wts["w_o"],
              wts["w_r"], wts["b_r"]]
    in_specs += [_const_spec(c.shape) for c in consts]
    args += consts

    h2_width = d + ROUTER_LANES if h2_with_route else d
    out_shape = [jax.ShapeDtypeStruct((nb, s, d), F32),
                 jax.ShapeDtypeStruct((nb, s, h2_width), h2_dtype),
                 jax.ShapeDtypeStruct((nb, s, ROUTER_LANES), F32),
                 jax.ShapeDtypeStruct((nb * n_seg, HIST_PAD, d), F32),
                 jax.ShapeDtypeStruct((SUBLANES, ROUTER_LANES), F32)]
    out_specs = [pl.BlockSpec((None, tile, d), lambda i, t: (i, t, 0)),
                 pl.BlockSpec((None, tile, h2_width), lambda i, t: (i, t, 0)),
                 pl.BlockSpec((None, tile, ROUTER_LANES), lambda i, t: (i, t, 0)),
                 pl.BlockSpec((n_seg, HIST_PAD, d), lambda i, t: (i, 0, 0)),
                 pl.BlockSpec((SUBLANES, ROUTER_LANES), lambda i, t: (0, 0))]
    if emit_v:
        out_shape.append(jax.ShapeDtypeStruct((nb, s, d), F32))
        out_specs.append(pl.BlockSpec((None, tile, d), lambda i, t: (i, t, 0)))
    scratch = [pltpu.VMEM((SUBLANES, ROUTER_LANES), F32)]
    if not has_hist:
        scratch.append(pltpu.VMEM((HIST_PAD, d), F32))

    return pl.pallas_call(
        functools.partial(_mixer_kernel, tile=tile, sub=sub, seg=seg, has_hist=has_hist,
                          emit_v=emit_v),
        grid=grid, in_specs=in_specs, out_specs=out_specs, out_shape=out_shape,
        scratch_shapes=scratch,
        compiler_params=pltpu.CompilerParams(
            dimension_semantics=("arbitrary", "arbitrary"),
            vmem_limit_bytes=VMEM_LIMIT_BYTES),
    )(*args)


def _moe_kernel(h2_ref, comb_ref, x1_ref, mod_ref, modf_ref, gfin_ref, wg_ref, wu_ref, wd_ref,
                y_ref, acc_ref, *, tile, seg):
    e = pl.program_id(2)
    d = x1_ref.shape[-1]
    n_seg = tile // seg

    @pl.when(e == 0)
    def _():
        acc_ref[...] = jnp.zeros_like(acc_ref)

    hb = h2_ref[...]
    hg = jnp.dot(hb, wg_ref[...], preferred_element_type=F32)
    hu = jnp.dot(hb, wu_ref[...], preferred_element_type=F32)
    he = (hg * _sigmoid(hg)) * hu
    ye = jnp.dot(he.astype(BF16), wd_ref[...], preferred_element_type=F32)
    lane = lax.broadcasted_iota(jnp.int32, (tile, ROUTER_LANES), 1)
    cw = jnp.sum(jnp.where(lane == e + EXPERT_LANE0, comb_ref[...], 0.0), axis=1, keepdims=True)
    acc_ref[...] += cw * ye

    @pl.when(e == N_EXPERTS - 1)
    def _():
        g2 = _segment_rows(mod_ref, 5 * d, 6 * d, n_seg, seg)
        shf = _segment_rows(modf_ref, 0, d, n_seg, seg)
        scf = _segment_rows(modf_ref, d, 2 * d, n_seg, seg)
        x2 = x1_ref[...] + g2 * acc_ref[...]
        y_ref[...] = _rms_norm(x2, gfin_ref[...]) * (1.0 + scf) + shf


def _moe(h2, comb, x1, mod, modf, wts, *, tile, seg):
    nb, s, d = x1.shape
    n_seg = tile // seg
    de = wts["w_gate"].shape[-1]
    tok = lambda i, t, e: (i, t, 0)
    per_seq = lambda i, t, e: (i, 0, 0)
    return pl.pallas_call(
        functools.partial(_moe_kernel, tile=tile, seg=seg),
        grid=(nb, s // tile, N_EXPERTS),
        in_specs=[pl.BlockSpec((None, tile, d), tok),
                  pl.BlockSpec((None, tile, ROUTER_LANES), tok),
                  pl.BlockSpec((None, tile, d), tok),
                  pl.BlockSpec((n_seg, 1, mod.shape[-1]), per_seq),
                  pl.BlockSpec((n_seg, 1, modf.shape[-1]), per_seq),
                  pl.BlockSpec((1, d), lambda i, t, e: (0, 0)),
                  pl.BlockSpec((None, d, de), lambda i, t, e: (e, 0, 0)),
                  pl.BlockSpec((None, d, de), lambda i, t, e: (e, 0, 0)),
                  pl.BlockSpec((None, de, d), lambda i, t, e: (e, 0, 0))],
        out_specs=pl.BlockSpec((None, tile, d), tok),
        out_shape=jax.ShapeDtypeStruct((nb, s, d), F32),
        scratch_shapes=[pltpu.VMEM((tile, d), F32)],
        compiler_params=pltpu.CompilerParams(
            dimension_semantics=("arbitrary", "arbitrary", "arbitrary"),
            vmem_limit_bytes=VMEM_LIMIT_BYTES),
    )(h2, comb, x1, mod, modf, wts["g_final"], wts["w_gate"], wts["w_up"], wts["w_down"])


def _plan_kernel(route_ref, start_ref, slot_ref, next_ref, *, tile):
    @pl.when(pl.program_id(0) == 0)
    def _():
        next_ref[...] = start_ref[...]

    bucket_row = route_ref[...].T[BUCKET_LANE:BUCKET_LANE + 1, :]
    rows = lax.broadcasted_iota(jnp.int32, (ROUTER_LANES, tile), 0).astype(F32)
    member = rows == bucket_row
    upper = (lax.broadcasted_iota(jnp.int32, (tile, tile), 0)
             <= lax.broadcasted_iota(jnp.int32, (tile, tile), 1))
    seen = jnp.dot(jnp.where(member, 1.0, 0.0).astype(BF16), jnp.where(upper, 1.0, 0.0).astype(BF16),
                   preferred_element_type=F32)
    nxt = next_ref[...]
    slot = jnp.sum(jnp.where(member, seen - 1.0 + nxt[:, 0:1], 0.0), axis=0, keepdims=True)
    slot_ref[...] = slot.astype(jnp.int32)
    next_ref[...] = nxt + seen[:, tile - 1:tile]


def _plan(route, bucket_start, *, tile):
    n = route.shape[0]
    start = jnp.broadcast_to(bucket_start[:, None], (ROUTER_LANES, ROUTER_LANES))
    slots = pl.pallas_call(
        functools.partial(_plan_kernel, tile=tile),
        grid=(n // tile,),
        in_specs=[pl.BlockSpec((tile, ROUTER_LANES), lambda t: (t, 0)),
                  pl.BlockSpec((ROUTER_LANES, ROUTER_LANES), lambda t: (0, 0))],
        out_specs=pl.BlockSpec((None, 1, tile), lambda t: (t, 0, 0)),
        out_shape=jax.ShapeDtypeStruct((n // tile, 1, tile), jnp.int32),
        scratch_shapes=[pltpu.VMEM((ROUTER_LANES, ROUTER_LANES), F32)],
        compiler_params=pltpu.CompilerParams(dimension_semantics=("arbitrary",)),
    )(route, start)
    return slots.reshape(n)


def _dispatch_kernel(slot_ref, pad_tile_ref, src_ref, dst_hbm, zeros_ref, sem, *, tile, pad_rows):
    t = pl.program_id(0)

    @pl.when(t == 0)
    def _():
        zeros_ref[...] = jnp.zeros_like(zeros_ref)
        for phase in ("start", "wait"):
            for b in range(2 * N_BUCKETS):
                @pl.when(pad_tile_ref[b] >= 0)
                def _():
                    first_row = pl.multiple_of(pad_tile_ref[b], pad_rows)
                    copy = pltpu.make_async_copy(
                        zeros_ref, dst_hbm.at[pl.ds(first_row, pad_rows)], sem)
                    copy.start() if phase == "start" else copy.wait()

    base = t * tile
    for r in range(tile):
        pltpu.make_async_copy(src_ref.at[pl.ds(r, 1)], dst_hbm.at[pl.ds(slot_ref[base + r], 1)],
                              sem).start()
    pltpu.make_async_copy(src_ref, dst_hbm.at[pl.ds(0, tile)], sem).wait()


def _dispatch(slots, pad_tile, src, n_slots, *, tile, pad_rows):
    n, d = src.shape
    return pl.pallas_call(
        functools.partial(_dispatch_kernel, tile=tile, pad_rows=pad_rows),
        grid_spec=pltpu.PrefetchScalarGridSpec(
            num_scalar_prefetch=2, grid=(n // tile,),
            in_specs=[pl.BlockSpec((tile, d), lambda t, s, p: (t, 0))],
            out_specs=pl.BlockSpec(memory_space=pl.ANY),
            scratch_shapes=[pltpu.VMEM((pad_rows, d), src.dtype), pltpu.SemaphoreType.DMA(())]),
        out_shape=jax.ShapeDtypeStruct((n_slots, d), src.dtype),
        compiler_params=pltpu.CompilerParams(dimension_semantics=("arbitrary",),
                                             vmem_limit_bytes=VMEM_LIMIT_BYTES),
    )(slots, pad_tile, src)


def _grouped_moe_kernel(ea_ref, eb_ref, n_used_ref, hs_ref, wga_ref, wua_ref, wda_ref,
                        wgb_ref, wub_ref, wdb_ref, ys_ref):
    del ea_ref, eb_ref
    d = ys_ref.shape[-1]
    j = pl.program_id(0)

    @pl.when(j < n_used_ref[0])
    def _():
        xb = hs_ref[:, 0:d].astype(BF16)
        record = hs_ref[:, d:d + ROUTER_LANES]
        acc = None
        for lane, (wg, wu, wd) in ((WA_LANE, (wga_ref, wua_ref, wda_ref)),
                                   (WB_LANE, (wgb_ref, wub_ref, wdb_ref))):
            g = jnp.dot(xb, wg[...], preferred_element_type=F32)
            u = jnp.dot(xb, wu[...], preferred_element_type=F32)
            he = (g * _sigmoid(g)) * u
            ye = jnp.dot(he.astype(BF16), wd[...], preferred_element_type=F32)
            term = record[:, lane:lane + 1] * ye
            acc = term if acc is None else acc + term
        ys_ref[...] = acc

    @pl.when(j >= n_used_ref[0])
    def _():
        ys_ref[...] = jnp.zeros_like(ys_ref)


def _grouped_moe(hs, tile_ea, tile_eb, n_used, wts, *, tile):
    n_slots, width = hs.shape
    d = width - ROUTER_LANES
    de = wts["w_gate"].shape[-1]
    lower = lambda j, ea, eb, nu: (ea[j], 0, 0)
    higher = lambda j, ea, eb, nu: (eb[j], 0, 0)
    w_specs = lambda idx: [pl.BlockSpec((None, d, de), idx), pl.BlockSpec((None, d, de), idx),
                           pl.BlockSpec((None, de, d), idx)]
    w_args = [wts["w_gate"], wts["w_up"], wts["w_down"]]
    return pl.pallas_call(
        _grouped_moe_kernel,
        grid_spec=pltpu.PrefetchScalarGridSpec(
            num_scalar_prefetch=3, grid=(n_slots // tile,),
            in_specs=[pl.BlockSpec((tile, width),
                                   lambda j, ea, eb, nu: (jnp.minimum(j, nu[0] - 1), 0))]
                     + w_specs(lower) + w_specs(higher),
            out_specs=pl.BlockSpec((tile, d), lambda j, ea, eb, nu: (j, 0))),
        out_shape=jax.ShapeDtypeStruct((n_slots, d), F32),
        compiler_params=pltpu.CompilerParams(dimension_semantics=("arbitrary",),
                                             vmem_limit_bytes=VMEM_LIMIT_BYTES),
    )(tile_ea, tile_eb, n_used, hs, *w_args, *w_args)


def _combine_kernel(slot_ref, x1_ref, mod_ref, modf_ref, gfin_ref, ys_hbm, y_ref,
                    ybuf, sems, *, tile, n_steps):
    t = pl.program_id(0)
    d = x1_ref.shape[-1]

    def fetch(step, buf):
        for r in range(tile):
            pltpu.make_async_copy(ys_hbm.at[pl.ds(slot_ref[step * tile + r], 1)],
                                  ybuf.at[buf, pl.ds(r, 1)], sems.at[buf]).start()

    def wait(buf):
        pltpu.make_async_copy(ys_hbm.at[pl.ds(0, tile)], ybuf.at[buf], sems.at[buf]).wait()

    @pl.when(t == 0)
    def _():
        fetch(0, 0)

    def step(buf):
        fetch((t + 1) % n_steps, 1 - buf)
        wait(buf)
        x2 = x1_ref[...] + mod_ref[0, :, 5 * d:6 * d] * ybuf[buf]
        y_ref[...] = (_rms_norm(x2, gfin_ref[...]) * (1.0 + modf_ref[0, :, d:2 * d])
                      + modf_ref[0, :, 0:d])

    for buf in range(2):
        pl.when(t % 2 == buf)(functools.partial(step, buf))

    @pl.when(t == n_steps - 1)
    def _():
        wait(n_steps % 2)


def _combine(slots, x1, mod, modf, ys, wts, *, tile, tiles_per_seq):
    n, d = x1.shape
    n_steps = n // tile
    tok = lambda t, s: (t, 0)
    per_seq = lambda t, s: (t // tiles_per_seq, 0, 0)
    return pl.pallas_call(
        functools.partial(_combine_kernel, tile=tile, n_steps=n_steps),
        grid_spec=pltpu.PrefetchScalarGridSpec(
            num_scalar_prefetch=1, grid=(n_steps,),
            in_specs=[pl.BlockSpec((tile, d), tok),
                      pl.BlockSpec((1, 1, mod.shape[-1]), per_seq),
                      pl.BlockSpec((1, 1, modf.shape[-1]), per_seq),
                      pl.BlockSpec((1, d), lambda t, s: (0, 0)),
                      pl.BlockSpec(memory_space=pl.ANY)],
            out_specs=pl.BlockSpec((tile, d), tok),
            scratch_shapes=[pltpu.VMEM((2, tile, d), F32), pltpu.SemaphoreType.DMA((2,))]),
        out_shape=jax.ShapeDtypeStruct((n, d), F32),
        compiler_params=pltpu.CompilerParams(dimension_semantics=("arbitrary",),
                                             vmem_limit_bytes=VMEM_LIMIT_BYTES),
    )(slots, x1, mod, modf, wts["g_final"], ys)


def _sorted_moe(h2, x1, route, counts, mod, modf, wts, *, seq_len):
    n, d = x1.shape
    tile = SORT_TILE
    n_tiles = n // tile + N_BUCKETS
    cnt = counts[:N_BUCKETS].astype(jnp.int32)
    padded = (cnt + tile - 1) // tile * tile
    ends = jnp.cumsum(padded)
    starts = ends - padded
    n_used = ends[-1] // tile
    tile_start = jnp.arange(n_tiles, dtype=jnp.int32) * tile
    tile_bucket = jnp.sum((ends[None, :] <= tile_start[:, None]).astype(jnp.int32), axis=1)
    last_bucket = tile_bucket[n_used - 1]
    tile_bucket = jnp.where(jnp.arange(n_tiles) < n_used, tile_bucket, last_bucket)
    experts = jnp.asarray(_BUCKET_EXPERTS)[tile_bucket]
    bucket_start = jnp.zeros((ROUTER_LANES,), F32).at[:N_BUCKETS].set(starts.astype(F32))

    tail_tile = n_used + jnp.arange(N_BUCKETS, dtype=jnp.int32)
    pad_tile = jnp.concatenate([jnp.where(padded > cnt, ends - tile, -1),
                                jnp.where(tail_tile < n_tiles, tail_tile * tile, -1)])

    slots = _plan(route, bucket_start, tile=min(PLAN_TILE, n))
    hs = _dispatch(slots, pad_tile, h2, n_tiles * tile, tile=min(DISPATCH_TILE, n), pad_rows=tile)
    ys = _grouped_moe(hs, experts[:, 0], experts[:, 1], n_used.reshape(1), wts, tile=tile)
    combine_tile = min(COMBINE_TILE, seq_len)
    return _combine(slots, x1, mod, modf, ys, wts, tile=combine_tile,
                    tiles_per_seq=seq_len // combine_tile)


def _layer_weights(l,w_in, b_in, conv_w, conv_b, ln_g, ln_b, w_a, v_ln_g, v_ln_b, w_s, b_s,
                   w_b, w_o, g_norm1, g_norm2, w_rg, b_rg, w_re, b_re, w_gate, w_up, w_down,
                   g_final):
    d = w_in.shape[1]
    gd = d // GMLP_GROUPS
    zero_row = jnp.zeros((1, d), F32)
    vecs = jnp.stack([conv_b[l], ln_g[l], ln_b[l], v_ln_g[l], v_ln_b[l], g_norm1[l], g_norm2[l],
                      zero_row[0]])
    taps = jnp.concatenate([conv_w[l][::-1], jnp.zeros((CONV_TAPS_PAD - CONV_WIDTH, d), F32)])
    taps = jnp.broadcast_to(taps[:, None, :], (CONV_TAPS_PAD, SUBLANES, d))
    pad = ROUTER_LANES - N_GROUPS - N_EXPERTS
    w_r = jnp.concatenate([w_rg[l], w_re[l], jnp.zeros((d, pad), F32)], axis=1)
    b_r = jnp.concatenate([b_rg[l], b_re[l], jnp.zeros((pad,), F32)]).reshape(1, ROUTER_LANES)
    return dict(
        vecs=vecs, taps=taps,
        w_in=w_in[l].astype(BF16), b_in=b_in[l].reshape(1, -1),
        w_a=w_a[l].astype(BF16), w_b=w_b[l].astype(BF16), w_o=w_o[l].astype(BF16),
        w_s=w_s[l], b_s=jnp.repeat(b_s[l].T, gd, axis=1),
        w_r=w_r.astype(BF16), b_r=b_r,
        w_gate=w_gate[l].astype(BF16), w_up=w_up[l].astype(BF16), w_down=w_down[l].astype(BF16),
        g_final=g_final.reshape(1, d))


def kernel(x_prompt, x_sample, cache_conv, c_prompt, c_sample, w_ada, b_ada, g_norm1, g_norm2, w_in, b_in, conv_w, conv_b, ln_g, ln_b, w_a, v_ln_g, v_ln_b, w_s, b_s, w_b, w_o, w_rg, b_rg, w_re, b_re, w_gate, w_up, w_down, g_final, w_ada_f, b_ada_f):
    depth = w_in.shape[0]
    assert depth == 1, "the streaming step is written for a single layer"
    l = 0
    batch, seq, d = x_prompt.shape
    dec_batch, dec_seq, _ = x_sample.shape
    assert dec_seq == HIST_PAD and seq % PROMPT_TILE == 0 and (dec_batch * dec_seq) % MIXER_TILE == 0
    assert (batch * seq) % min(DISPATCH_TILE, batch * seq) == 0 and seq % min(COMBINE_TILE, seq) == 0

    wts = _layer_weights(l, w_in, b_in, conv_w, conv_b, ln_g, ln_b, w_a, v_ln_g, v_ln_b, w_s, b_s,
                         w_b, w_o, g_norm1, g_norm2, w_rg, b_rg, w_re, b_re, w_gate, w_up, w_down,
                         g_final)

    c_all = jnp.concatenate([c_prompt, c_sample], axis=0)
    mod = _modulation(c_all, w_ada[l], b_ada[l])[:, None, :]
    modf = _modulation(c_all, w_ada_f, b_ada_f)[:, None, :]

    x1p, h2p, routep, histp, counts = _mixer(x_prompt, mod[:batch], None, wts, tile=PROMPT_TILE,
                                             sub=MIXER_TILE, seg=PROMPT_TILE, emit_v=False,
                                             h2_dtype=F32, h2_with_route=True)
    n_prompt = batch * seq
    y_prompt = _sorted_moe(h2p.reshape(n_prompt, d + ROUTER_LANES), x1p.reshape(n_prompt, d),
                           routep.reshape(n_prompt, ROUTER_LANES), counts[0], mod[:batch],
                           modf[:batch], wts, seq_len=seq).reshape(batch, seq, d)

    n_tok = dec_batch * dec_seq
    xs = x_sample.reshape(n_tok // MIXER_TILE, MIXER_TILE, d)
    hist_s = jnp.pad(cache_conv[l], ((0, 0), (HIST_PAD - HIST, 0), (0, 0)))
    x1s, h2s, combs, hists, _, vs = _mixer(xs, mod[batch:], hist_s, wts, tile=MIXER_TILE,
                                           sub=MIXER_TILE, seg=dec_seq, emit_v=True, h2_dtype=BF16,
                                           h2_with_route=False)
    moe_tile_s = min(MOE_TILE, n_tok)
    y_sample = _moe(h2s.reshape(n_tok // moe_tile_s, moe_tile_s, d),
                    combs.reshape(n_tok // moe_tile_s, moe_tile_s, ROUTER_LANES),
                    x1s.reshape(n_tok // moe_tile_s, moe_tile_s, d),
                    mod[batch:], modf[batch:], wts, tile=moe_tile_s, seg=dec_seq)

    state_conv_prompt = histp[None, :, HIST_PAD - HIST:, :]
    state_conv_sample = hists[None, :, HIST_PAD - HIST:, :]
    state_gmlp_v_sample = vs.reshape(1, dec_batch, dec_seq, d)
    return (y_prompt, y_sample.reshape(dec_batch, dec_seq, d), state_conv_prompt,
            state_conv_sample, state_gmlp_v_sample)
```
